```python
import math
import jax
import jax.numpy as jnp
from jax import lax
import numpy as np

D_MODEL = 2048
BATCH = 4
SEQ = 2048
DEPTH = 4
DEC_BATCH = 128
DEC_SEQ = 4
PAST_LEN = 8192
PAGE_SIZE = 128

N_MIXERS = 3
N_A_LAYERS = (DEPTH + 2) // 3
N_B_LAYERS = (DEPTH + 1) // 3
N_C_LAYERS = DEPTH // 3
HEAD_DIM = 128
Q_BLOCK = 128
RMS_EPS = 1e-6
NEG = -1e30
FORCE_SCORE = 1e9
N_BUCKETS = 32
MAX_DISTANCE = 128
BIAS_HEADS = 16
A_HEADS = 16
A_KV_HEADS = 2
IDX_HEADS = 16
IDX_DIM = 64
IDX_TOPK = 256
B_HEADS = 16
B_NOPE = 128
B_ROPE = 64
B_V = 128
B_LORA = 512
ROPE_THETA = 10000.0
C_HEADS = 16
C_GROUPS = 2
CMP_LEN = 32
CMP_STRIDE = 16
SLC_LEN = 64
SLC_TOPN = 16
WINDOW = 512
D_FF = 4096
CONV_W = 3
A_IN = A_HEADS * HEAD_DIM + 2 * A_KV_HEADS * HEAD_DIM + IDX_HEADS * IDX_DIM + IDX_DIM + IDX_HEADS
B_IN = B_HEADS * (B_NOPE + B_ROPE) + B_LORA + B_ROPE
C_IN = C_HEADS * HEAD_DIM + 6 * C_GROUPS * HEAD_DIM + 3 * C_HEADS

kernel_name = 'hybrid_dsa_mla_nsa_convffn_step'


def rmsnorm(x, g):
    xf = x.astype(jnp.float32)
    y = xf * lax.rsqrt(jnp.mean(xf * xf, axis=-1, keepdims=True) + RMS_EPS)
    return (y * g.astype(jnp.float32)).astype(x.dtype)


def split_cols(z, sizes):
    cuts = [int(c) for c in np.cumsum(sizes)[:-1]]
    return jnp.split(z, cuts, axis=-1)


def t5_bucket(dist):
    n = jnp.maximum(dist, 0)
    exact = N_BUCKETS // 2
    nf = jnp.maximum(n, 1).astype(jnp.float32)
    large = exact + (jnp.log(nf / exact) / math.log(MAX_DISTANCE / exact) * (N_BUCKETS - exact)).astype(jnp.int32)
    return jnp.where(n < exact, n, jnp.minimum(large, N_BUCKETS - 1))


def rope(x, pos):
    half = x.shape[-1] // 2
    inv = ROPE_THETA ** (-jnp.arange(half, dtype=jnp.float32) / half)
    ang = pos.astype(jnp.float32)[:, None] * inv[None, :]
    ang = ang.reshape((ang.shape[0],) + (1,) * (x.ndim - 3) + (half,))
    cos, sin = jnp.cos(ang), jnp.sin(ang)
    xf = x.astype(jnp.float32)
    x1, x2 = xf[..., :half], xf[..., half:]
    return jnp.concatenate([x1 * cos - x2 * sin, x1 * sin + x2 * cos], axis=-1).astype(x.dtype)


def masked_softmax(s, mask):
    p = jax.nn.softmax(jnp.where(mask, s.astype(jnp.float32), NEG), axis=-1)
    return p * mask


def batch_index(n, ndim):
    return jnp.arange(n).reshape((n,) + (1,) * (ndim - 1))


def to_blocks(x):
    b, t = x.shape[:2]
    return x.reshape((b, t // Q_BLOCK, Q_BLOCK) + x.shape[2:]).swapaxes(0, 1)


def from_blocks(x):
    n, b = x.shape[:2]
    return x.swapaxes(0, 1).reshape((b, n * x.shape[2]) + x.shape[3:])


def gather_pages(pool, slot, page_table):
    g = pool[slot, page_table]
    return g.reshape((g.shape[0], g.shape[1] * g.shape[2]) + g.shape[3:])


def prompt_rows(rows, pos, gidx=None):
    b = batch_index(rows.shape[0], pos.ndim)
    pos = jnp.clip(pos, 0, rows.shape[1] - 1)
    if gidx is None:
        return rows[b, pos]
    return rows[b, pos, gidx]


def sample_rows(pool, slot, page_table, new, pos, gidx=None):
    past = page_table.shape[1] * PAGE_SIZE
    b = batch_index(new.shape[0], pos.ndim)
    p_old = jnp.clip(pos, 0, past - 1)
    p_new = jnp.clip(pos - past, 0, new.shape[1] - 1)
    phys = page_table[b, p_old // PAGE_SIZE]
    off = p_old % PAGE_SIZE
    if gidx is None:
        old, cur = pool[slot, phys, off], new[b, p_new]
    else:
        old, cur = pool[slot, phys, off, gidx], new[b, p_new, gidx]
    is_old = (pos < past)[(Ellipsis,) + (None,) * (old.ndim - pos.ndim)]
    return jnp.where(is_old, old, cur)


def a_project(h, w_in):
    b, t, _ = h.shape
    q, k, v, qi, ki, wi = split_cols(h @ w_in, [A_HEADS * HEAD_DIM, A_KV_HEADS * HEAD_DIM, A_KV_HEADS * HEAD_DIM,
                                                IDX_HEADS * IDX_DIM, IDX_DIM, IDX_HEADS])
    return (q.reshape(b, t, A_HEADS, HEAD_DIM), k.reshape(b, t, A_KV_HEADS, HEAD_DIM),
            v.reshape(b, t, A_KV_HEADS, HEAD_DIM), qi.reshape(b, t, IDX_HEADS, IDX_DIM), ki,
            wi * IDX_HEADS ** -0.5)


def dsa_core(q, q_idx, w_idx, k_idx, q_pos, n_keep, fetch, rel_bias):
    b, tq = q.shape[:2]
    n_keys = k_idx.shape[1]
    dots = jnp.einsum('bqhd,bsd->bqhs', q_idx, k_idx).astype(jnp.float32)
    score = jnp.einsum('bqhs,bqh->bqs', jax.nn.relu(dots), w_idx.astype(jnp.float32))
    causal = jnp.arange(n_keys)[None, :] <= q_pos[:, None]
    score = jnp.where(causal[None], score, NEG)
    _, sel = lax.top_k(score, n_keep)
    valid = sel <= q_pos[None, :, None]
    k_s, v_s = fetch(sel)
    rep = A_HEADS // A_KV_HEADS
    qg = q.reshape(b, tq, A_KV_HEADS, rep, HEAD_DIM)
    logits = jnp.einsum('bqngd,bqknd->bqngk', qg, k_s).astype(jnp.float32) * HEAD_DIM ** -0.5
    bias = rel_bias[t5_bucket(q_pos[None, :, None] - sel)]
    bias = bias.reshape(b, tq, n_keep, A_KV_HEADS, rep).transpose(0, 1, 3, 4, 2)
    p = masked_softmax(logits + bias, valid[:, :, None, None, :])
    out = jnp.einsum('bqngk,bqknd->bqngd', p.astype(v_s.dtype), v_s)
    return out.reshape(b, tq, A_HEADS * HEAD_DIM)


def a_prompt(h, w_in, w_o, rel_bias):
    b, t, _ = h.shape
    q, k, v, qi, ki, wi = a_project(h, w_in)
    n_keep = min(IDX_TOPK, t // 4)

    def blk(args):
        qb, qib, wib, pos = args
        fetch = lambda sel: (prompt_rows(k, sel), prompt_rows(v, sel))
        return dsa_core(qb, qib, wib, ki, pos, n_keep, fetch, rel_bias)

    xs = (to_blocks(q), to_blocks(qi), to_blocks(wi), jnp.arange(t).reshape(-1, Q_BLOCK))
    o = from_blocks(lax.map(blk, xs))
    return o @ w_o, (k, v, ki)


def a_sample(h, slot, cache_k, cache_v, cache_idx, page_table, w_in, w_o, rel_bias):
    b, t, _ = h.shape
    past = page_table.shape[1] * PAGE_SIZE
    q, k, v, qi, ki, wi = a_project(h, w_in)
    k_idx = jnp.concatenate([gather_pages(cache_idx, slot, page_table).astype(ki.dtype), ki], axis=1)
    n_keep = min(IDX_TOPK, (past + t) // 4)
    pos = past + jnp.arange(t)
    fetch = lambda sel: (sample_rows(cache_k, slot, page_table, k, sel),
                         sample_rows(cache_v, slot, page_table, v, sel))
    o = dsa_core(q, qi, wi, k_idx, pos, n_keep, fetch, rel_bias)
    return o @ w_o, (k, v, ki)


def mla_project(h, pos, w_in, kv_norm):
    b, t, _ = h.shape
    qf, c, kr = split_cols(h @ w_in, [B_HEADS * (B_NOPE + B_ROPE), B_LORA, B_ROPE])
    qf = qf.reshape(b, t, B_HEADS, B_NOPE + B_ROPE)
    q_nope = qf[..., :B_NOPE]
    q_rope = rope(qf[..., B_NOPE:], pos)
    return q_nope, q_rope, rmsnorm(c, kv_norm), rope(kr, pos)


def split_kvb(w_kvb):
    w = w_kvb.reshape(B_LORA, B_HEADS, B_NOPE + B_V)
    return w[..., :B_NOPE], w[..., B_NOPE:]


def mla_core(q_nope, q_rope, c, kr, q_pos, k_pos, w_uk, w_uv):
    q_lat = jnp.einsum('bqhd,chd->bqhc', q_nope, w_uk)
    s = (jnp.einsum('bqhc,bkc->bhqk', q_lat, c) + jnp.einsum('bqhr,bkr->bhqk', q_rope, kr)).astype(jnp.float32)
    s = s * (B_NOPE + B_ROPE) ** -0.5
    p = masked_softmax(s, (k_pos[None, :] <= q_pos[:, None])[None, None])
    o_lat = jnp.einsum('bhqk,bkc->bqhc', p.astype(c.dtype), c)
    o = jnp.einsum('bqhc,chd->bqhd', o_lat, w_uv)
    return o.reshape(o.shape[0], o.shape[1], B_HEADS * B_V)


def b_prompt(h, w_in, kv_norm, w_kvb, w_o):
    b, t, _ = h.shape
    pos = jnp.arange(t)
    q_nope, q_rope, c, kr = mla_project(h, pos, w_in, kv_norm)
    w_uk, w_uv = split_kvb(w_kvb)

    def blk(args):
        qn, qr, qp = args
        return mla_core(qn, qr, c, kr, qp, pos, w_uk, w_uv)

    o = from_blocks(lax.map(blk, (to_blocks(q_nope), to_blocks(q_rope), pos.reshape(-1, Q_BLOCK))))
    return o @ w_o, (c, kr)


def b_sample(h, slot, cache_lat, cache_rope, page_table, w_in, kv_norm, w_kvb, w_o):
    b, t, _ = h.shape
    past = page_table.shape[1] * PAGE_SIZE
    pos = past + jnp.arange(t)
    q_nope, q_rope, c, kr = mla_project(h, pos, w_in, kv_norm)
    c_all = jnp.concatenate([gather_pages(cache_lat, slot, page_table).astype(c.dtype), c], axis=1)
    kr_all = jnp.concatenate([gather_pages(cache_rope, slot, page_table).astype(kr.dtype), kr], axis=1)
    w_uk, w_uv = split_kvb(w_kvb)
    o = mla_core(q_nope, q_rope, c_all, kr_all, pos, jnp.arange(past + t), w_uk, w_uv)
    return o @ w_o, (c, kr)


def nsa_project(h, w_in, gate_b):
    b, t, _ = h.shape
    sizes = [C_HEADS * HEAD_DIM] + [C_GROUPS * HEAD_DIM] * 6 + [3 * C_HEADS]
    q, kc, vc, ks, vs, kw, vw, g = split_cols(h @ w_in, sizes)
    kv = tuple(a.reshape(b, t, C_GROUPS, HEAD_DIM) for a in (kc, vc, ks, vs, kw, vw))
    gates = jax.nn.sigmoid((g + gate_b).astype(jnp.float32)).reshape(b, t, C_HEADS, 3)
    return q.reshape(b, t, C_HEADS, HEAD_DIM), kv, gates


def compress(rows, w_pos):
    b, n_rows = rows.shape[:2]
    n_sub = -(-n_rows // CMP_STRIDE)
    rows = jnp.pad(rows, ((0, 0), (0, n_sub * CMP_STRIDE - n_rows), (0, 0), (0, 0)))
    sub = rows.reshape(b, n_sub, CMP_STRIDE, C_GROUPS, HEAD_DIM)
    first = jnp.einsum('bnigd,id->bngd', sub, w_pos[:CMP_STRIDE])
    second = jnp.einsum('bnigd,id->bngd', sub, w_pos[CMP_STRIDE:])
    blocks = (first[:, :-1] + second[:, 1:]) / CMP_LEN
    ends = jnp.arange(n_sub - 1) * CMP_STRIDE + CMP_LEN - 1
    return blocks, ends


def cmp_to_slc(n_cmp, n_slc):
    c0 = jnp.arange(n_cmp)[:, None] * CMP_STRIDE
    s0 = jnp.arange(n_slc)[None, :] * SLC_LEN
    shared = jnp.minimum(c0 + CMP_LEN, s0 + SLC_LEN) - jnp.maximum(c0, s0)
    return jnp.maximum(shared, 0).astype(jnp.float32) / CMP_LEN


def gqa_block_attend(q, k, v, dist, mask, rel_bias):
    b, tq = q.shape[:2]
    n_k = k.shape[1]
    rep = C_HEADS // C_GROUPS
    qg = q.reshape(b, tq, C_GROUPS, rep, HEAD_DIM)
    s = jnp.einsum('bqgrd,bkgd->bgrqk', qg, k).astype(jnp.float32) * HEAD_DIM ** -0.5
    bias = rel_bias[t5_bucket(dist)].reshape(tq, n_k, C_GROUPS, rep).transpose(2, 3, 0, 1)
    p = masked_softmax(s + bias, mask)
    o = jnp.einsum('bgrqk,bkgd->bqgrd', p.astype(v.dtype), v)
    return o, p


def nsa_core(q, gates, q_pos, kc, vc, c_end, n_slc, fetch_slc, kw, vw, w_pos, rel_bias):
    b, tq = q.shape[:2]
    rep = C_HEADS // C_GROUPS
    o_cmp, p_cmp = gqa_block_attend(q, kc, vc, q_pos[:, None] - c_end[None, :],
                                    c_end[None, :] <= q_pos[:, None], rel_bias)
    imp = jnp.einsum('bgrqc,cs->bgqs', p_cmp, cmp_to_slc(kc.shape[1], n_slc))
    blk = jnp.arange(n_slc)[None, :]
    forced = (blk == (q_pos // SLC_LEN)[:, None]) | (blk == 0)
    admissible = blk * SLC_LEN <= q_pos[:, None]
    imp = jnp.where(forced, FORCE_SCORE, jnp.where(admissible, imp, NEG))
    n_top = min(SLC_TOPN, n_slc)
    _, sel = lax.top_k(imp, n_top)
    pos = (sel[..., None] * SLC_LEN + jnp.arange(SLC_LEN)).reshape(b, C_GROUPS, tq, n_top * SLC_LEN)
    ks, vs = fetch_slc(pos)
    qg = q.reshape(b, tq, C_GROUPS, rep, HEAD_DIM)
    s = jnp.einsum('bqgrd,bgqnd->bgrqn', qg, ks).astype(jnp.float32) * HEAD_DIM ** -0.5
    gidx = jnp.arange(C_GROUPS)[None, :, None, None]
    tbl = rel_bias.reshape(N_BUCKETS, C_GROUPS, rep)
    bias = tbl[t5_bucket(q_pos[None, None, :, None] - pos), gidx].transpose(0, 1, 4, 2, 3)
    valid = (pos <= q_pos[None, None, :, None])[:, :, None]
    p = masked_softmax(s + bias, valid)
    o_slc = jnp.einsum('bgrqn,bgqnd->bqgrd', p.astype(vs.dtype), vs)
    wmask = (w_pos[None, :] <= q_pos[:, None]) & (w_pos[None, :] > q_pos[:, None] - WINDOW) & (w_pos[None, :] >= 0)
    o_win, _ = gqa_block_attend(q, kw, vw, q_pos[:, None] - w_pos[None, :], wmask, rel_bias)
    g = gates.astype(q.dtype).reshape(b, tq, C_GROUPS, rep, 3)
    o = o_cmp * g[..., 0:1] + o_slc * g[..., 1:2] + o_win * g[..., 2:3]
    return o.reshape(b, tq, C_HEADS * HEAD_DIM)


def c_prompt(h, w_in, gate_b, wk_pos, wv_pos, w_o, rel_bias):
    b, t, _ = h.shape
    q, (kc, vc, ks, vs, kw, vw), gates = nsa_project(h, w_in, gate_b)
    kcb, c_end = compress(kc, wk_pos)
    vcb, _ = compress(vc, wv_pos)
    n_slc = -(-t // SLC_LEN)
    gidx = jnp.arange(C_GROUPS)[None, :, None, None]
    pad = ((0, 0), (WINDOW, 0), (0, 0), (0, 0))
    kw_pad, vw_pad = jnp.pad(kw, pad), jnp.pad(vw, pad)

    def blk(args):
        qb, gb, qp, b0 = args
        kwb = lax.dynamic_slice_in_dim(kw_pad, b0, WINDOW + Q_BLOCK, axis=1)
        vwb = lax.dynamic_slice_in_dim(vw_pad, b0, WINDOW + Q_BLOCK, axis=1)
        w_pos = b0 - WINDOW + jnp.arange(WINDOW + Q_BLOCK)
        fetch = lambda p: (prompt_rows(ks, p, gidx), prompt_rows(vs, p, gidx))
        return nsa_core(qb, gb, qp, kcb, vcb, c_end, n_slc, fetch, kwb, vwb, w_pos, rel_bias)

    n_blk = t // Q_BLOCK
    xs = (to_blocks(q), to_blocks(gates), jnp.arange(t).reshape(n_blk, Q_BLOCK), jnp.arange(n_blk) * Q_BLOCK)
    o = from_blocks(lax.map(blk, xs))
    n_w = min(WINDOW, t)
    return o @ w_o, (kc, vc, ks, vs, kw[:, t - n_w:], vw[:, t - n_w:])


def c_sample(h, slot, cache_kc, cache_vc, cache_ks, cache_vs, win_k, win_v, page_table,
             w_in, gate_b, wk_pos, wv_pos, w_o, rel_bias):
    b, t, _ = h.shape
    past = page_table.shape[1] * PAGE_SIZE
    q, (kc, vc, ks, vs, kw, vw), gates = nsa_project(h, w_in, gate_b)
    kcb, c_end = compress(jnp.concatenate([gather_pages(cache_kc, slot, page_table).astype(kc.dtype), kc], axis=1), wk_pos)
    vcb, _ = compress(jnp.concatenate([gather_pages(cache_vc, slot, page_table).astype(vc.dtype), vc], axis=1), wv_pos)
    n_slc = -(-(past + t) // SLC_LEN)
    pos = past + jnp.arange(t)
    gidx = jnp.arange(C_GROUPS)[None, :, None, None]
    fetch = lambda p: (sample_rows(cache_ks, slot, page_table, ks, p, gidx),
                       sample_rows(cache_vs, slot, page_table, vs, p, gidx))
    kw_all = jnp.concatenate([win_k[slot].astype(kw.dtype), kw], axis=1)
    vw_all = jnp.concatenate([win_v[slot].astype(vw.dtype), vw], axis=1)
    n_buf = win_k.shape[2]
    w_pos = past - n_buf + jnp.arange(n_buf + t)
    o = nsa_core(q, gates, pos, kcb, vcb, c_end, n_slc, fetch, kw_all, vw_all, w_pos, rel_bias)
    n_w = min(WINDOW, n_buf + t)
    return o @ w_o, (kc, vc, ks, vs, kw_all[:, -n_w:], vw_all[:, -n_w:])


def conv_ffn(h, prev, w_up, conv_w, conv_b, w_down):
    t = h.shape[1]
    g, u = jnp.split(h @ w_up, 2, axis=-1)
    gx = jnp.concatenate([prev.astype(g.dtype), g], axis=1)
    conv = conv_b
    for i in range(CONV_W):
        conv = conv + gx[:, i:i + t] * conv_w[i]
    return (jax.nn.silu(conv) * u) @ w_down, gx[:, -(CONV_W - 1):]


def setup_inputs(seed: int = 0) -> dict:
    key = jax.random.key(seed)
    keys = iter(jax.random.split(key, 40))
    f32 = jnp.float32

    def nrm(shape, scale=1.0):
        return jax.random.normal(next(keys), shape, f32) * scale

    n_pages = PAST_LEN // PAGE_SIZE
    n_used = DEC_BATCH * n_pages
    n_pool = n_used + max(1, n_used // 4)
    n_win = min(WINDOW, PAST_LEN)
    d = D_MODEL
    x_prompt = nrm((BATCH, SEQ, d))
    x_sample = nrm((DEC_BATCH, DEC_SEQ, d))
    cache_a_k = nrm((N_A_LAYERS, n_pool, PAGE_SIZE, A_KV_HEADS, HEAD_DIM))
    cache_a_v = nrm((N_A_LAYERS, n_pool, PAGE_SIZE, A_KV_HEADS, HEAD_DIM))
    cache_a_idx = nrm((N_A_LAYERS, n_pool, PAGE_SIZE, IDX_DIM))
    cache_b_latent = nrm((N_B_LAYERS, n_pool, PAGE_SIZE, B_LORA))
    cache_b_rope = nrm((N_B_LAYERS, n_pool, PAGE_SIZE, B_ROPE))
    cache_c_cmp_k = nrm((N_C_LAYERS, n_pool, PAGE_SIZE, C_GROUPS, HEAD_DIM))
    cache_c_cmp_v = nrm((N_C_LAYERS, n_pool, PAGE_SIZE, C_GROUPS, HEAD_DIM))
    cache_c_slc_k = nrm((N_C_LAYERS, n_pool, PAGE_SIZE, C_GROUPS, HEAD_DIM))
    cache_c_slc_v = nrm((N_C_LAYERS, n_pool, PAGE_SIZE, C_GROUPS, HEAD_DIM))
    state_c_win_k = nrm((N_C_LAYERS, DEC_BATCH, n_win, C_GROUPS, HEAD_DIM))
    state_c_win_v = nrm((N_C_LAYERS, DEC_BATCH, n_win, C_GROUPS, HEAD_DIM))
    state_ffn_conv = nrm((DEPTH, DEC_BATCH, CONV_W - 1, D_FF))
    page_table = jax.random.permutation(next(keys), n_pool)[:n_used].reshape(DEC_BATCH, n_pages).astype(jnp.int32)
    return {
        'x_prompt': x_prompt,
        'x_sample': x_sample,
        'cache_a_k': cache_a_k,
        'cache_a_v': cache_a_v,
        'cache_a_idx': cache_a_idx,
        'cache_b_latent': cache_b_latent,
        'cache_b_rope': cache_b_rope,
        'cache_c_cmp_k': cache_c_cmp_k,
        'cache_c_cmp_v': cache_c_cmp_v,
        'cache_c_slc_k': cache_c_slc_k,
        'cache_c_slc_v': cache_c_slc_v,
        'state_c_win_k': state_c_win_k,
        'state_c_win_v': state_c_win_v,
        'state_ffn_conv': state_ffn_conv,
        'page_table': page_table,
        'rel_bias': nrm((N_BUCKETS, BIAS_HEADS), 0.5),
        'attn_norm': 1.0 + nrm((DEPTH, d), 0.02),
        'ffn_norm': 1.0 + nrm((DEPTH, d), 0.02),
        'final_norm': 1.0 + nrm((d,), 0.02),
        'a_w_in': nrm((N_A_LAYERS, d, A_IN), d ** -0.5),
        'a_w_o': nrm((N_A_LAYERS, A_HEADS * HEAD_DIM, d), (A_HEADS * HEAD_DIM) ** -0.5),
        'b_w_in': nrm((N_B_LAYERS, d, B_IN), d ** -0.5),
        'b_kv_norm': 1.0 + nrm((N_B_LAYERS, B_LORA), 0.02),
        'b_w_kvb': nrm((N_B_LAYERS, B_LORA, B_HEADS * (B_NOPE + B_V)), B_LORA ** -0.5),
        'b_w_o': nrm((N_B_LAYERS, B_HEADS * B_V, d), (B_HEADS * B_V) ** -0.5),
        'c_w_in': nrm((N_C_LAYERS, d, C_IN), d ** -0.5),
        'c_gate_b': nrm((N_C_LAYERS, 3 * C_HEADS), 0.1),
        'c_cmp_wk': 1.0 + nrm((N_C_LAYERS, CMP_LEN, HEAD_DIM), 0.1),
        'c_cmp_wv': 1.0 + nrm((N_C_LAYERS, CMP_LEN, HEAD_DIM), 0.1),
        'c_w_o': nrm((N_C_LAYERS, C_HEADS * HEAD_DIM, d), (C_HEADS * HEAD_DIM) ** -0.5),
        'ffn_w_up': nrm((DEPTH, d, 2 * D_FF), d ** -0.5),
        'ffn_conv_w': nrm((DEPTH, CONV_W, D_FF), CONV_W ** -0.5),
        'ffn_conv_b': nrm((DEPTH, D_FF), 0.02),
        'ffn_w_down': nrm((DEPTH, D_FF, d), D_FF ** -0.5),
    }


def reference(x_prompt, x_sample, cache_a_k, cache_a_v, cache_a_idx, cache_b_latent, cache_b_rope,
              cache_c_cmp_k, cache_c_cmp_v, cache_c_slc_k, cache_c_slc_v, state_c_win_k, state_c_win_v,
              state_ffn_conv, page_table, rel_bias, attn_norm, ffn_norm, final_norm,
              a_w_in, a_w_o, b_w_in, b_kv_norm, b_w_kvb, b_w_o,
              c_w_in, c_gate_b, c_cmp_wk, c_cmp_wv, c_w_o,
              ffn_w_up, ffn_conv_w, ffn_conv_b, ffn_w_down):
    hp, hs = x_prompt, x_sample
    n_prompt = hp.shape[0]
    a_p, a_s = ([], [], []), ([], [], [])
    b_p, b_s = ([], []), ([], [])
    c_p, c_s = ([], [], [], [], [], []), ([], [], [], [], [], [])
    conv_p, conv_s = [], []
    for i in range(DEPTH):
        kind, slot = i % N_MIXERS, i // N_MIXERS
        np_, ns_ = rmsnorm(hp, attn_norm[i]), rmsnorm(hs, attn_norm[i])
        if kind == 0:
            yp, st_p = a_prompt(np_, a_w_in[slot], a_w_o[slot], rel_bias)
            ys, st_s = a_sample(ns_, slot, cache_a_k, cache_a_v, cache_a_idx, page_table,
                                a_w_in[slot], a_w_o[slot], rel_bias)
            dst_p, dst_s = a_p, a_s
        elif kind == 1:
            yp, st_p = b_prompt(np_, b_w_in[slot], b_kv_norm[slot], b_w_kvb[slot], b_w_o[slot])
            ys, st_s = b_sample(ns_, slot, cache_b_latent, cache_b_rope, page_table,
                                b_w_in[slot], b_kv_norm[slot], b_w_kvb[slot], b_w_o[slot])
            dst_p, dst_s = b_p, b_s
        else:
            yp, st_p = c_prompt(np_, c_w_in[slot], c_gate_b[slot], c_cmp_wk[slot], c_cmp_wv[slot],
                                c_w_o[slot], rel_bias)
            ys, st_s = c_sample(ns_, slot, cache_c_cmp_k, cache_c_cmp_v, cache_c_slc_k, cache_c_slc_v,
                                state_c_win_k, state_c_win_v, page_table, c_w_in[slot], c_gate_b[slot],
                                c_cmp_wk[slot], c_cmp_wv[slot], c_w_o[slot], rel_bias)
            dst_p, dst_s = c_p, c_s
        for lst, a in zip(dst_p, st_p):
            lst.append(a)
        for lst, a in zip(dst_s, st_s):
            lst.append(a)
        hp, hs = hp + yp, hs + ys
        fp, cp = conv_ffn(rmsnorm(hp, ffn_norm[i]), jnp.zeros((n_prompt, CONV_W - 1, D_FF), hp.dtype),
                          ffn_w_up[i], ffn_conv_w[i], ffn_conv_b[i], ffn_w_down[i])
        fs, cs = conv_ffn(rmsnorm(hs, ffn_norm[i]), state_ffn_conv[i],
                          ffn_w_up[i], ffn_conv_w[i], ffn_conv_b[i], ffn_w_down[i])
        hp, hs = hp + fp, hs + fs
        conv_p.append(cp)
        conv_s.append(cs)
    y_prompt = rmsnorm(hp, final_norm)
    y_sample = rmsnorm(hs, final_norm)
    st = jnp.stack
    return (y_prompt, y_sample,
            st(a_p[0]), st(a_p[1]), st(a_p[2]), st(b_p[0]), st(b_p[1]),
            st(c_p[0]), st(c_p[1]), st(c_p[2]), st(c_p[3]), st(c_p[4]), st(c_p[5]), st(conv_p),
            st(a_s[0]), st(a_s[1]), st(a_s[2]), st(b_s[0]), st(b_s[1]),
            st(c_s[0]), st(c_s[1]), st(c_s[2]), st(c_s[3]), st(c_s[4]), st(c_s[5]), st(conv_s))
```

```python
import functools
import math

import jax
import jax.numpy as jnp
import numpy as np
from jax import lax
from jax.experimental import pallas as pl
from jax.experimental.pallas import tpu as pltpu

F32 = jnp.float32
BF16 = jnp.bfloat16
I32 = jnp.int32

D_MODEL = 2048
DEPTH = 4
PAGE_SIZE = 128
HEAD_DIM = 128
RMS_EPS = 1e-6
NEG = -1e30
FORCE_SCORE = 1e9
N_BUCKETS = 32
MAX_DISTANCE = 128
A_HEADS = 16
A_KV_HEADS = 2
IDX_HEADS = 16
IDX_DIM = 64
IDX_TOPK = 256
B_HEADS = 16
B_NOPE = 128
B_ROPE = 64
B_V = 128
B_LORA = 512
ROPE_THETA = 10000.0
C_HEADS = 16
C_GROUPS = 2
CMP_LEN = 32
CMP_STRIDE = 16
SLC_LEN = 64
SLC_TOPN = 16
WINDOW = 512
D_FF = 4096
CONV_W = 3

V7X_LANES = 128
V7X_SUBLANES = 8
V7X_VMEM_BYTES = 64 * 1024 * 1024
VMEM_LIMIT_BYTES = V7X_VMEM_BYTES - 8 * 1024 * 1024

W_IN_PAD = 3840
TAIL_COL = 3584
ROW_TILE_DEC = 8
MLA_QK = 640
BIAS_SAT = 113
MINF = -3.0e38


def _cparams(sem):
    return pltpu.CompilerParams(dimension_semantics=sem, vmem_limit_bytes=VMEM_LIMIT_BYTES)


def _round_up(x, m):
    return (x + m - 1) // m * m


def _mm_kernel(*refs, has_norm, has_res, cast_a):
    it = iter(refs)
    a_ref = next(it)
    g_ref = next(it) if has_norm else None
    w_ref = next(it)
    r_ref = next(it) if has_res else None
    o_ref = next(it)
    abf_ref = next(it) if cast_a else None
    if cast_a:
        @pl.when(pl.program_id(1) == 0)
        def _():
            a = a_ref[...].astype(F32)
            if has_norm:
                a = a * lax.rsqrt(jnp.mean(a * a, axis=-1, keepdims=True) + RMS_EPS) * g_ref[...]
            abf_ref[...] = a.astype(BF16)
        a = abf_ref[...]
    else:
        a = a_ref[...]
    acc = jnp.dot(a, w_ref[...].astype(BF16), preferred_element_type=F32)
    if has_res:
        acc = acc + r_ref[...]
    o_ref[...] = acc.astype(o_ref.dtype)


def _mm(a, w, *, norm_g=None, res=None, tn=512, name="mm"):
    m, k = a.shape
    n = w.shape[1]
    tm = min(m, 1024)
    assert m % tm == 0 and n % tn == 0
    has_norm, has_res = norm_g is not None, res is not None
    cast_a = has_norm or a.dtype != BF16
    ins = [a]
    specs = [pl.BlockSpec((tm, k), lambda i, j: (i, 0))]
    if has_norm:
        ins.append(norm_g.reshape(1, k))
        specs.append(pl.BlockSpec((1, k), lambda i, j: (0, 0)))
    ins.append(w)
    specs.append(pl.BlockSpec((k, tn), lambda i, j: (0, j)))
    if has_res:
        ins.append(res)
        specs.append(pl.BlockSpec((tm, tn), lambda i, j: (i, j)))
    return pl.pallas_call(
        functools.partial(_mm_kernel, has_norm=has_norm, has_res=has_res, cast_a=cast_a),
        out_shape=jax.ShapeDtypeStruct((m, n), F32),
        grid=(m // tm, n // tn),
        in_specs=specs,
        out_specs=pl.BlockSpec((tm, tn), lambda i, j: (i, j)),
        scratch_shapes=[pltpu.VMEM((tm, k), BF16)] if cast_a else [],
        compiler_params=_cparams(("parallel", "arbitrary")),
        name=name,
    )(*ins)


def _rmsnorm_kernel(x_ref, g_ref, o_ref):
    x = x_ref[...]
    o_ref[...] = x * lax.rsqrt(jnp.mean(x * x, axis=-1, keepdims=True) + RMS_EPS) * g_ref[...]


def _rmsnorm(x, g):
    m, d = x.shape
    tm = min(m, 1024)
    return pl.pallas_call(
        _rmsnorm_kernel,
        out_shape=jax.ShapeDtypeStruct((m, d), F32),
        grid=(m // tm,),
        in_specs=[pl.BlockSpec((tm, d), lambda i: (i, 0)), pl.BlockSpec((1, d), lambda i: (0, 0))],
        out_specs=pl.BlockSpec((tm, d), lambda i: (i, 0)),
        compiler_params=_cparams(("parallel",)),
        name="final_norm",
    )(x, g.reshape(1, d))


def _ffn_up_kernel(*refs, tm, blocks_per_seq, decode):
    if decode:
        x_ref, gn_ref, wg_ref, wu_ref, cw_ref, cb_ref, p1_ref, p2_ref, h_ref, g_ref, xn_ref = refs
    else:
        x_ref, gn_ref, wg_ref, wu_ref, cw_ref, cb_ref, h_ref, tail_ref, xn_ref, carry_ref = refs
    i = pl.program_id(0)
    j = pl.program_id(1)

    @pl.when(j == 0)
    def _():
        x = x_ref[...]
        xn_ref[...] = (x * lax.rsqrt(jnp.mean(x * x, axis=-1, keepdims=True) + RMS_EPS) * gn_ref[...]).astype(BF16)

    xn = xn_ref[...]
    g = jnp.dot(xn, wg_ref[...].astype(BF16), preferred_element_type=F32)
    u = jnp.dot(xn, wu_ref[...].astype(BF16), preferred_element_type=F32)
    row = lax.broadcasted_iota(I32, g.shape, 0)
    r1 = pltpu.roll(g, 1, axis=0)
    r2 = pltpu.roll(g, 2, axis=0)
    if decode:
        t = row & 3
        gm1 = jnp.where(t >= 1, r1, p1_ref[...])
        gm2 = jnp.where(t >= 2, r2, p2_ref[...])
        g_ref[...] = g
    else:
        tf = g.shape[1]
        @pl.when(i == 0)
        def _():
            carry_ref[j] = jnp.zeros((V7X_SUBLANES, tf), F32)

        c = jnp.where(i % blocks_per_seq == 0, 0.0, carry_ref[j])
        c1 = c[V7X_SUBLANES - 1:V7X_SUBLANES, :]
        c2 = c[V7X_SUBLANES - 2:V7X_SUBLANES - 1, :]
        gm1 = jnp.where(row == 0, c1, r1)
        gm2 = jnp.where(row == 0, c2, jnp.where(row == 1, c1, r2))
        tail = g[tm - V7X_SUBLANES:, :]
        carry_ref[j] = tail
        tail_ref[0] = tail
    cw = cw_ref[...]
    conv = cb_ref[...] + gm2 * cw[0:1, :] + gm1 * cw[1:2, :] + g * cw[2:3, :]
    h = conv * (1.0 / (1.0 + jnp.exp(-conv))) * u
    h_ref[...] = h.astype(BF16)


def _ffn_up(x, gn, w_up, conv_w, conv_b, *, seq_len, prev=None, tf=512):
    m, d = x.shape
    nf = D_FF // tf
    decode = prev is not None
    tm = min(m, 1024)
    ins = [x, gn.reshape(1, d), w_up, w_up, conv_w, conv_b.reshape(1, D_FF)]
    specs = [
        pl.BlockSpec((tm, d), lambda i, j: (i, 0)),
        pl.BlockSpec((1, d), lambda i, j: (0, 0)),
        pl.BlockSpec((d, tf), lambda i, j: (0, j)),
        pl.BlockSpec((d, tf), lambda i, j: (0, j + nf)),
        pl.BlockSpec((CONV_W, tf), lambda i, j: (0, j)),
        pl.BlockSpec((1, tf), lambda i, j: (0, j)),
    ]
    if decode:
        assert seq_len == 4 and m == tm
        zero = jnp.zeros_like(prev[:, :1])
        p1 = jnp.concatenate([prev[:, 1:2], zero, zero, zero], axis=1).reshape(m, D_FF)
        p2 = jnp.concatenate([prev[:, 0:1], prev[:, 1:2], zero, zero], axis=1).reshape(m, D_FF)
        ins += [p1, p2]
        specs += [pl.BlockSpec((tm, tf), lambda i, j: (i, j))] * 2
        out_shape = (jax.ShapeDtypeStruct((m, D_FF), BF16), jax.ShapeDtypeStruct((m, D_FF), F32))
        out_specs = (pl.BlockSpec((tm, tf), lambda i, j: (i, j)), pl.BlockSpec((tm, tf), lambda i, j: (i, j)))
        scratch = [pltpu.VMEM((tm, d), BF16)]
        bps = 1
    else:
        assert seq_len % tm == 0
        bps = seq_len // tm
        out_shape = (jax.ShapeDtypeStruct((m, D_FF), BF16), jax.ShapeDtypeStruct((m // tm, V7X_SUBLANES, D_FF), F32))
        out_specs = (pl.BlockSpec((tm, tf), lambda i, j: (i, j)), pl.BlockSpec((1, V7X_SUBLANES, tf), lambda i, j: (i, 0, j)))
        scratch = [pltpu.VMEM((tm, d), BF16), pltpu.VMEM((nf, V7X_SUBLANES, tf), F32)]
    return pl.pallas_call(
        functools.partial(_ffn_up_kernel, tm=tm, blocks_per_seq=bps, decode=decode),
        out_shape=out_shape,
        grid=(m // tm, nf),
        in_specs=specs,
        out_specs=out_specs,
        scratch_shapes=scratch,
        compiler_params=_cparams(("arbitrary", "arbitrary")),
        name="ffn_up_dec" if decode else "ffn_up",
    )(*ins)


def _fa_chunk(i, kc, *, mode, tq, tk, qpos0, kpos0, n_chunks):
    if mode == "window":
        first = (qpos0 + i * tq - WINDOW + 1 - kpos0) // tk
        return first + kc
    return kc


def _fa_chunk_clamped(i, kc, **kw):
    c = _fa_chunk(i, kc, **kw)
    if kw["mode"] != "window":
        last = (kw["qpos0"] + i * kw["tq"] + kw["tq"] - 1 - kw["kpos0"]) // kw["tk"]
        c = jnp.minimum(c, last)
    return jnp.clip(c, 0, kw["n_chunks"] - 1)


def _fa_kernel(*refs, hg, dq, dv, tq, tk, nkc, scale, mode, qpos0, kpos0, n_chunks,
               has_mask, has_bias, has_vproj):
    it = iter(refs)
    q_ref, k_ref, v_ref = next(it), next(it), next(it)
    mask_ref = next(it) if has_mask else None
    bias_ref = next(it) if has_bias else None
    wv_ref = next(it) if has_vproj else None
    o_ref, qs_ref, m_ref, acc_ref = next(it), next(it), next(it), next(it)
    i = pl.program_id(1)
    n = pl.program_id(2)
    kc = pl.program_id(3)
    t0 = qpos0 + i * tq
    geo = dict(mode=mode, tq=tq, tk=tk, qpos0=qpos0, kpos0=kpos0, n_chunks=n_chunks)

    @pl.when(kc == 0)
    def _():
        for h in range(hg):
            qs_ref[h * tq:(h + 1) * tq, :] = q_ref[0, :, h * dq:(h + 1) * dq].astype(BF16)
        m_ref[...] = jnp.full(m_ref.shape, NEG, F32)
        acc_ref[...] = jnp.zeros(acc_ref.shape, F32)

    craw = _fa_chunk(i, kc, **geo)
    s0 = kpos0 + craw * tk
    needed = (craw >= 0) & (craw < n_chunks) & (s0 <= t0 + tq - 1)

    @pl.when(needed)
    def _():
        kb = k_ref[0].astype(BF16)
        s = lax.dot_general(qs_ref[...], kb, (((1,), (1,)), ((), ())), preferred_element_type=F32) * scale
        if has_bias:
            parts = []
            for u in range(tk // V7X_LANES):
                dsel = jnp.clip(t0 // V7X_LANES - (s0 // V7X_LANES + u), 0, 2)
                parts.append(bias_ref[n, dsel])
            s = s + (parts[0] if len(parts) == 1 else jnp.concatenate(parts, axis=1))
        add = None
        if has_mask:
            add = mask_ref[0, 0, 0]
        if mode in ("causal", "window"):
            qp = t0 + lax.broadcasted_iota(I32, (tq, tk), 0)
            kp = s0 + lax.broadcasted_iota(I32, (tq, tk), 1)
            ok = kp <= qp
            if mode == "window":
                ok = ok & (kp > qp - WINDOW) & (kp >= 0)
            add = jnp.where(ok, 0.0, NEG)
        if add is not None:
            s = s + (add if hg == 1 else jnp.concatenate([add] * hg, axis=0))
        m_old = m_ref[...]
        m_new = jnp.maximum(m_old, jnp.max(s, axis=1, keepdims=True))
        alpha = jnp.exp(m_old - m_new)
        p = jnp.exp(s - jnp.concatenate([m_new] * (tk // V7X_LANES), axis=1))
        vb = v_ref[0].astype(BF16)
        vext = jnp.concatenate([vb, jnp.ones((tk, V7X_LANES), BF16)], axis=1)
        pv = jnp.dot(p.astype(BF16), vext, preferred_element_type=F32)
        acc_ref[...] = acc_ref[...] * jnp.concatenate([alpha] * (dv // V7X_LANES + 1), axis=1) + pv
        m_ref[...] = m_new

    @pl.when(kc == nkc - 1)
    def _():
        valid = m_ref[...] > 0.5 * NEG
        l = jnp.where(valid, acc_ref[:, dv:dv + V7X_LANES], 1.0)
        inv = jnp.where(valid, 1.0 / l, 0.0)
        o = acc_ref[:, :dv] * jnp.concatenate([inv] * (dv // V7X_LANES), axis=1)
        for h in range(hg):
            oh = o[h * tq:(h + 1) * tq, :]
            if has_vproj:
                oh = jnp.dot(oh.astype(BF16), wv_ref[h], preferred_element_type=F32)
            o_ref[0, :, h * HEAD_DIM:(h + 1) * HEAD_DIM] = oh


def _fa(q, k, v, *, groups, hg, dq, dv, tq, tk, nkc, n_chunks, scale, mode, qpos0=0, kpos0=0,
        q_col=lambda n: n, k_col=lambda n: n, v_col=lambda n: n, mask=None, bias=None, vproj=None, name="fa"):
    b, tq_total = q.shape[0], q.shape[1]
    nq = tq_total // tq
    geo = dict(mode=mode, tq=tq, tk=tk, qpos0=qpos0, kpos0=kpos0, n_chunks=n_chunks)
    kmap = lambda col: (lambda bb, i, n, kc: (bb, _fa_chunk_clamped(i, kc, **geo), col(n)))
    ins = [q, k, v]
    specs = [
        pl.BlockSpec((1, tq, hg * dq), lambda bb, i, n, kc: (bb, i, q_col(n))),
        pl.BlockSpec((1, tk, dq), kmap(k_col)),
        pl.BlockSpec((1, tk, dv), kmap(v_col)),
    ]
    if mask is not None:
        ins.append(mask)
        gsel = (lambda n: n) if mask.shape[1] > 1 else (lambda n: 0)
        specs.append(pl.BlockSpec((1, 1, 1, tq, tk), lambda bb, i, n, kc: (bb, gsel(n), _fa_chunk_clamped(i, kc, **geo), i, 0)))
    if bias is not None:
        ins.append(bias)
        specs.append(pl.BlockSpec(bias.shape, lambda bb, i, n, kc: (0, 0, 0, 0)))
    if vproj is not None:
        ins.append(vproj)
        specs.append(pl.BlockSpec(vproj.shape, lambda bb, i, n, kc: (0, 0, 0)))
    r = hg * tq
    return pl.pallas_call(
        functools.partial(_fa_kernel, hg=hg, dq=dq, dv=dv, tq=tq, tk=tk, nkc=nkc, scale=scale, mode=mode,
                          qpos0=qpos0, kpos0=kpos0, n_chunks=n_chunks, has_mask=mask is not None,
                          has_bias=bias is not None, has_vproj=vproj is not None),
        out_shape=jax.ShapeDtypeStruct((b, tq_total, groups * hg * HEAD_DIM), F32),
        grid=(b, nq, groups, nkc),
        in_specs=specs,
        out_specs=pl.BlockSpec((1, tq, hg * HEAD_DIM), lambda bb, i, n, kc: (bb, i, n)),
        scratch_shapes=[pltpu.VMEM((r, dq), BF16), pltpu.VMEM((r, V7X_LANES), F32), pltpu.VMEM((r, dv + V7X_LANES), F32)],
        compiler_params=_cparams(("parallel", "parallel", "arbitrary", "arbitrary")),
        name=name,
    )(*ins)


def _sortable(x):
    b = lax.bitcast_convert_type(x, I32)
    return b ^ ((b >> 31) & jnp.int32(0x7FFFFFFF))


def _dsa_select_kernel(qa_ref, qb_ref, wi_ref, kidx_ref, o_ref, key_ref, qbf_ref, *, tq, tk, n_chunks, n_keep, qpos0):
    i = pl.program_id(1)
    t0 = qpos0 + i * tq
    nck = jnp.minimum((t0 + tq - 1) // tk + 1, n_chunks)
    half = IDX_HEADS * IDX_DIM // 2
    qbf_ref[:, :half] = qa_ref[0].astype(BF16)
    qbf_ref[:, half:] = qb_ref[0].astype(BF16)
    w = wi_ref[0][:, IDX_DIM:IDX_DIM + IDX_HEADS] * (IDX_HEADS ** -0.5)
    qpos = t0 + lax.broadcasted_iota(I32, (tq, tk), 0)
    lane = lax.broadcasted_iota(I32, (tq, tk), 1)

    def score_chunk(kc, carry):
        kic = kidx_ref[0, pl.ds(pl.multiple_of(kc * tk, tk), tk), :][:, :IDX_DIM].astype(BF16)
        acc = jnp.zeros((tq, tk), F32)
        for h in range(IDX_HEADS):
            d = lax.dot_general(qbf_ref[:, h * IDX_DIM:(h + 1) * IDX_DIM], kic, (((1,), (1,)), ((), ())),
                                preferred_element_type=F32)
            acc = acc + jnp.maximum(d, 0.0) * w[:, h:h + 1]
        sc = jnp.where(kc * tk + lane <= qpos, acc + 0.0, NEG)
        key_ref[kc] = _sortable(sc)
        return carry

    lax.fori_loop(0, nck, score_chunk, 0)

    def count(pred):
        def body(kc, c):
            return c + jnp.sum(jnp.where(pred(key_ref[kc], kc), 1.0, 0.0), axis=1, keepdims=True)
        return lax.fori_loop(0, nck, body, jnp.zeros((tq, 1), F32))

    neg_key = jnp.int32(np.array(NEG, np.float32).view(np.int32) ^ 0x7FFFFFFF)
    extra = ((n_chunks - nck) * tk).astype(F32)
    kf = float(n_keep)

    def count_ge(cand):
        return count(lambda kk, kc: kk >= cand) + jnp.where(cand <= neg_key, extra, 0.0)

    tau = jnp.where(count_ge(jnp.zeros((tq, 1), I32)) >= kf, 0, jnp.iinfo(jnp.int32).min).astype(I32)
    for bit in range(30, -1, -1):
        cand = tau | jnp.int32(1 << bit)
        tau = jnp.where(count_ge(cand) >= kf, cand, tau)
    n_ge = count_ge(tau)
    ties = jnp.max(jnp.where(n_ge != kf, 1.0, 0.0)) > 0.5

    def write(sel_fn):
        def body(kc, carry):
            kk = key_ref[kc]
            ok = sel_fn(kk, kc) & (kc * tk + lane <= qpos)
            o_ref[0, 0, kc] = jnp.where(ok, 0.0, NEG)
            return carry
        lax.fori_loop(0, nck, body, 0)

        def fill(kc, carry):
            o_ref[0, 0, kc] = jnp.full((tq, tk), NEG, F32)
            return carry
        lax.fori_loop(nck, n_chunks, fill, 0)

    @pl.when(jnp.logical_not(ties))
    def _():
        write(lambda kk, kc: kk >= tau)

    @pl.when(ties)
    def _():
        r = kf - count(lambda kk, kc: kk > tau) - jnp.where(tau < neg_key, extra, 0.0)
        nbits = int(math.ceil(math.log2(n_chunks * tk)))
        mth = jnp.zeros((tq, 1), I32)
        for bit in range(nbits - 1, -1, -1):
            cand = mth | jnp.int32(1 << bit)
            c = count(lambda kk, kc: (kk == tau) & (kc * tk + lane < cand))
            c = c + jnp.where(tau == neg_key, jnp.clip(cand - nck * tk, 0, (n_chunks - nck) * tk).astype(F32), 0.0)
            mth = jnp.where(c < r, cand, mth)
        write(lambda kk, kc: (kk > tau) | ((kk == tau) & (kc * tk + lane <= mth)))


def _dsa_select(zq, kidx, *, tq, tk, n_keep, qpos0, kidx_col):
    b, tq_total = zq.shape[0], zq.shape[1]
    tkeys = kidx.shape[1]
    n_chunks = tkeys // tk
    nq = tq_total // tq
    half = IDX_HEADS * IDX_DIM // 2
    qi_col = (A_HEADS * HEAD_DIM + 2 * A_KV_HEADS * HEAD_DIM) // half
    kw = kidx.shape[2] if kidx_col is None else V7X_LANES
    return pl.pallas_call(
        functools.partial(_dsa_select_kernel, tq=tq, tk=tk, n_chunks=n_chunks, n_keep=n_keep, qpos0=qpos0),
        out_shape=jax.ShapeDtypeStruct((b, 1, n_chunks, tq_total, tk), F32),
        grid=(b, nq),
        in_specs=[
            pl.BlockSpec((1, tq, half), lambda bb, i: (bb, i, qi_col)),
            pl.BlockSpec((1, tq, half), lambda bb, i: (bb, i, qi_col + 1)),
            pl.BlockSpec((1, tq, V7X_LANES), lambda bb, i: (bb, i, TAIL_COL // V7X_LANES)),
            pl.BlockSpec((1, tkeys, kw), lambda bb, i: (bb, 0, 0 if kidx_col is None else kidx_col)),
        ],
        out_specs=pl.BlockSpec((1, 1, n_chunks, tq, tk), lambda bb, i: (bb, 0, 0, i, 0)),
        scratch_shapes=[pltpu.VMEM((n_chunks, tq, tk), I32), pltpu.VMEM((tq, 2 * half), BF16)],
        compiler_params=_cparams(("parallel", "arbitrary")),
        name="dsa_select",
    )(zq, zq, zq, kidx)


def _rope_rows(x, cos, sin):
    half = x.shape[-1] // 2
    x1, x2 = x[:, :half], x[:, half:]
    return jnp.concatenate([x1 * cos - x2 * sin, x1 * sin + x2 * cos], axis=-1)


def _mla_prep_kernel(qf_ref, c_ref, kr_ref, gk_ref, wuk_ref, cos_ref, sin_ref, qcat_ref, ckr_ref, cout_ref, krout_ref):
    cos, sin = cos_ref[...], sin_ref[...]
    tq = cos.shape[0]
    c = c_ref[0]
    cn = c * lax.rsqrt(jnp.mean(c * c, axis=-1, keepdims=True) + RMS_EPS) * gk_ref[...]
    kr = _rope_rows(kr_ref[0][:, :B_ROPE], cos, sin)
    cout_ref[0] = cn
    krout_ref[0] = kr
    zpad = jnp.zeros((tq, MLA_QK - B_LORA - B_ROPE), BF16)
    ckr_ref[0] = jnp.concatenate([cn.astype(BF16), kr.astype(BF16), zpad], axis=1)
    hd = B_NOPE + B_ROPE
    for h in range(B_HEADS):
        qn = qf_ref[0, :, h * hd:h * hd + B_NOPE].astype(BF16)
        qr = _rope_rows(qf_ref[0, :, h * hd + B_NOPE:(h + 1) * hd], cos, sin)
        ql = jnp.dot(qn, wuk_ref[h], preferred_element_type=F32)
        qcat_ref[0, :, h * MLA_QK:(h + 1) * MLA_QK] = jnp.concatenate([ql.astype(BF16), qr.astype(BF16), zpad], axis=1)


def _mla_prep(z, kv_norm, wuk_t, cos, sin, *, tq):
    b, t = z.shape[0], z.shape[1]
    qw = B_HEADS * (B_NOPE + B_ROPE)
    half = B_ROPE // 2
    return pl.pallas_call(
        _mla_prep_kernel,
        out_shape=(
            jax.ShapeDtypeStruct((b, t, B_HEADS * MLA_QK), BF16),
            jax.ShapeDtypeStruct((b, t, MLA_QK), BF16),
            jax.ShapeDtypeStruct((b, t, B_LORA), F32),
            jax.ShapeDtypeStruct((b, t, B_ROPE), F32),
        ),
        grid=(b, t // tq),
        in_specs=[
            pl.BlockSpec((1, tq, qw), lambda bb, i: (bb, i, 0)),
            pl.BlockSpec((1, tq, B_LORA), lambda bb, i: (bb, i, qw // B_LORA)),
            pl.BlockSpec((1, tq, V7X_LANES), lambda bb, i: (bb, i, TAIL_COL // V7X_LANES)),
            pl.BlockSpec((1, B_LORA), lambda bb, i: (0, 0)),
            pl.BlockSpec(wuk_t.shape, lambda bb, i: (0, 0, 0)),
            pl.BlockSpec((tq, half), lambda bb, i: (i, 0)),
            pl.BlockSpec((tq, half), lambda bb, i: (i, 0)),
        ],
        out_specs=(
            pl.BlockSpec((1, tq, B_HEADS * MLA_QK), lambda bb, i: (bb, i, 0)),
            pl.BlockSpec((1, tq, MLA_QK), lambda bb, i: (bb, i, 0)),
            pl.BlockSpec((1, tq, B_LORA), lambda bb, i: (bb, i, 0)),
            pl.BlockSpec((1, tq, B_ROPE), lambda bb, i: (bb, i, 0)),
        ),
        compiler_params=_cparams(("parallel", "parallel")),
        name="mla_prep",
    )(z, z, z, kv_norm.reshape(1, B_LORA), wuk_t, cos, sin)


def _nsa_compress_kernel(xk_ref, xv_ref, wk1_ref, wk2_ref, wv1_ref, wv2_ref, ok_ref, ov_ref, *, n_sub, n_out):
    gw = C_GROUPS * HEAD_DIM
    for x_ref, w1_ref, w2_ref, o_ref in ((xk_ref, wk1_ref, wk2_ref, ok_ref), (xv_ref, wv1_ref, wv2_ref, ov_ref)):
        first = jnp.zeros((n_sub, gw), F32)
        second = jnp.zeros((n_sub, gw), F32)
        for r in range(CMP_STRIDE):
            xr = x_ref[0, :, r * gw:(r + 1) * gw]
            first = first + xr * w1_ref[:, r * gw:(r + 1) * gw]
            second = second + xr * w2_ref[:, r * gw:(r + 1) * gw]
        blocks = (first + pltpu.roll(second, n_sub - 1, axis=0)) * (1.0 / CMP_LEN)
        o_ref[0, :n_sub, :] = blocks
        if n_out > n_sub:
            o_ref[0, n_sub:, :] = jnp.zeros((n_out - n_sub, gw), F32)


def _nsa_compress(xk, xv, wk, wv, *, n_out):
    b, n_sub, width = xk.shape

    def flat(w):
        w2 = jnp.broadcast_to(w.reshape(2, CMP_STRIDE, 1, HEAD_DIM), (2, CMP_STRIDE, C_GROUPS, HEAD_DIM))
        w2 = w2.reshape(2, 1, width)
        return w2[0], w2[1]

    wk1, wk2 = flat(wk)
    wv1, wv2 = flat(wv)
    wspec = pl.BlockSpec((1, width), lambda bb: (0, 0))
    return pl.pallas_call(
        functools.partial(_nsa_compress_kernel, n_sub=n_sub, n_out=n_out),
        out_shape=(jax.ShapeDtypeStruct((b, n_out, C_GROUPS * HEAD_DIM), F32),) * 2,
        grid=(b,),
        in_specs=[pl.BlockSpec((1, n_sub, width), lambda bb: (bb, 0, 0))] * 2 + [wspec] * 4,
        out_specs=(pl.BlockSpec((1, n_out, C_GROUPS * HEAD_DIM), lambda bb: (bb, 0, 0)),) * 2,
        compiler_params=_cparams(("parallel",)),
        name="nsa_compress",
    )(xk, xv, wk1, wk2, wv1, wv2)


def _nsa_cmp_kernel(q_ref, kcb_ref, vcb_ref, bias_ref, m_ref, e_ref, o_ref, mask_ref, *, tq, tk, nb, ns_pad, n_slc, n_top, tkeys, qpos0):
    i = pl.program_id(1)
    t0 = qpos0 + i * tq
    hg = C_HEADS // C_GROUPS
    scale = HEAD_DIM ** -0.5
    r = hg * tq
    qrow = t0 + lax.broadcasted_iota(I32, (tq, nb), 0)
    cend = lax.broadcasted_iota(I32, (tq, nb), 1) * CMP_STRIDE + (CMP_LEN - 1)
    vis = jnp.where(cend <= qrow, 0.0, NEG)
    qs = t0 + lax.broadcasted_iota(I32, (tq, ns_pad), 0)
    blk = lax.broadcasted_iota(I32, (tq, ns_pad), 1)
    forced = (blk == qs // SLC_LEN) | (blk == 0)
    admissible = blk * SLC_LEN <= qs
    qk = t0 + lax.broadcasted_iota(I32, (tq, tkeys), 0)
    kpos = lax.broadcasted_iota(I32, (tq, tkeys), 1)
    for g in range(C_GROUPS):
        qg = jnp.concatenate([q_ref[0, :, (g * hg + h) * HEAD_DIM:(g * hg + h + 1) * HEAD_DIM] for h in range(hg)],
                             axis=0).astype(BF16)
        kg = kcb_ref[0, :, g * HEAD_DIM:(g + 1) * HEAD_DIM].astype(BF16)
        vg = vcb_ref[0, :, g * HEAD_DIM:(g + 1) * HEAD_DIM].astype(BF16)
        s = lax.dot_general(qg, kg, (((1,), (1,)), ((), ())), preferred_element_type=F32) * scale
        s = s + bias_ref[g, 0] + jnp.concatenate([vis] * hg, axis=0)
        mx = jnp.max(s, axis=1, keepdims=True)
        e = jnp.exp(s - mx)
        valid = mx > 0.5 * NEG
        p = e * jnp.where(valid, 1.0 / jnp.sum(e, axis=1, keepdims=True), 0.0)
        o = jnp.dot(p.astype(BF16), vg, preferred_element_type=F32)
        psum = p[0:tq]
        for h in range(hg):
            o_ref[0, :, (g * hg + h) * HEAD_DIM:(g * hg + h + 1) * HEAD_DIM] = o[h * tq:(h + 1) * tq]
            if h:
                psum = psum + p[h * tq:(h + 1) * tq]
        p_hi = psum.astype(BF16)
        p_lo = (psum - p_hi.astype(F32)).astype(BF16)
        imp = jnp.dot(p_hi, m_ref[...], preferred_element_type=F32) + jnp.dot(p_lo, m_ref[...], preferred_element_type=F32)
        imp = jnp.where(forced, FORCE_SCORE, jnp.where(admissible, imp, NEG))
        imp = jnp.where(blk < n_slc, imp, MINF)
        rank = jnp.zeros((tq, ns_pad), F32)
        for c in range(n_slc):
            col = imp[:, c:c + 1]
            rank = rank + jnp.where((col > imp) | ((col == imp) & (blk > c)), 1.0, 0.0)
        sel = jnp.where(rank < n_top, 1.0, 0.0).astype(BF16)
        hit = jnp.dot(sel, e_ref[...], preferred_element_type=F32)
        msk = jnp.where((hit > 0.5) & (kpos <= qk), 0.0, NEG)
        for c in range(tkeys // tk):
            mask_ref[0, g, c] = msk[:, c * tk:(c + 1) * tk]


def _nsa_cmp(q, kcb, vcb, bias_c, *, tq, tk, n_slc, tkeys, qpos0, q_col=0):
    b, tq_total = q.shape[0], q.shape[1]
    nb = kcb.shape[1]
    ns_pad = _round_up(n_slc, V7X_LANES)
    c0 = np.arange(nb)[:, None] * CMP_STRIDE
    s0 = np.arange(ns_pad)[None, :] * SLC_LEN
    shared = np.minimum(c0 + CMP_LEN, s0 + SLC_LEN) - np.maximum(c0, s0)
    m = np.maximum(shared, 0).astype(np.float32) / CMP_LEN
    m[:, n_slc:] = 0.0
    e = (np.arange(tkeys)[None, :] // SLC_LEN == np.arange(ns_pad)[:, None]).astype(np.float32)
    hq = C_HEADS * HEAD_DIM
    return pl.pallas_call(
        functools.partial(_nsa_cmp_kernel, tq=tq, tk=tk, nb=nb, ns_pad=ns_pad, n_slc=n_slc, n_top=min(SLC_TOPN, n_slc),
                          tkeys=tkeys, qpos0=qpos0),
        out_shape=(jax.ShapeDtypeStruct((b, tq_total, hq), F32), jax.ShapeDtypeStruct((b, C_GROUPS, tkeys // tk, tq_total, tk), F32)),
        grid=(b, tq_total // tq),
        in_specs=[
            pl.BlockSpec((1, tq, hq), lambda bb, i: (bb, i, q_col)),
            pl.BlockSpec((1, nb, C_GROUPS * HEAD_DIM), lambda bb, i: (bb, 0, 0)),
            pl.BlockSpec((1, nb, C_GROUPS * HEAD_DIM), lambda bb, i: (bb, 0, 0)),
            pl.BlockSpec((C_GROUPS, 1, (C_HEADS // C_GROUPS) * tq, nb), lambda bb, i: (0, i, 0, 0)),
            pl.BlockSpec((nb, ns_pad), lambda bb, i: (0, 0)),
            pl.BlockSpec((ns_pad, tkeys), lambda bb, i: (0, 0)),
        ],
        out_specs=(
            pl.BlockSpec((1, tq, hq), lambda bb, i: (bb, i, 0)),
            pl.BlockSpec((1, C_GROUPS, tkeys // tk, tq, tk), lambda bb, i: (bb, 0, 0, i, 0)),
        ),
        compiler_params=_cparams(("parallel", "parallel")),
        name="nsa_cmp",
    )(q, kcb, vcb, bias_c, jnp.asarray(m, BF16), jnp.asarray(e, BF16))


def _nsa_combine_kernel(oc_ref, os_ref, ow_ref, g_ref, gb_ref, o_ref):
    g = g_ref[...][:, :3 * C_HEADS] + gb_ref[...]
    gate = 1.0 / (1.0 + jnp.exp(-g))
    for h in range(C_HEADS):
        sl = slice(h * HEAD_DIM, (h + 1) * HEAD_DIM)
        o_ref[:, sl] = (oc_ref[:, sl] * gate[:, 3 * h:3 * h + 1] + os_ref[:, sl] * gate[:, 3 * h + 1:3 * h + 2]
                        + ow_ref[:, sl] * gate[:, 3 * h + 2:3 * h + 3])


def _nsa_combine(o_cmp, o_slc, o_win, z, gate_b):
    m, d = o_cmp.shape
    tm = min(m, 512)
    ospec = pl.BlockSpec((tm, d), lambda i: (i, 0))
    return pl.pallas_call(
        _nsa_combine_kernel,
        out_shape=jax.ShapeDtypeStruct((m, d), F32),
        grid=(m // tm,),
        in_specs=[ospec, ospec, ospec, pl.BlockSpec((tm, V7X_LANES), lambda i: (i, TAIL_COL // V7X_LANES)),
                  pl.BlockSpec((1, 3 * C_HEADS), lambda i: (0, 0))],
        out_specs=ospec,
        compiler_params=_cparams(("parallel",)),
        name="nsa_combine",
    )(o_cmp, o_slc, o_win, z, gate_b.reshape(1, 3 * C_HEADS))


def _t5_bucket(dist):
    n = jnp.maximum(dist, 0)
    exact = N_BUCKETS // 2
    nf = jnp.maximum(n, 1).astype(F32)
    large = exact + (jnp.log(nf / exact) / math.log(MAX_DISTANCE / exact) * (N_BUCKETS - exact)).astype(I32)
    return jnp.where(n < exact, n, jnp.minimum(large, N_BUCKETS - 1))


def _bias_by_distance(rel_bias):
    return rel_bias[_t5_bucket(jnp.arange(2 * V7X_LANES))]


def _bias_tiles(f, tq, groups):
    a = jnp.arange(tq)[:, None]
    c = jnp.arange(V7X_LANES)[None, :]
    t0 = f[jnp.clip(a - c, 0, 2 * V7X_LANES - 1)]
    t1 = f[V7X_LANES + a - c]
    t2 = jnp.broadcast_to(f[2 * V7X_LANES - 1], t0.shape)
    tiles = jnp.stack([t0, t1, t2], axis=0)
    heads = f.shape[1]
    tiles = tiles.transpose(3, 0, 1, 2).reshape(groups, heads // groups, 3, tq, V7X_LANES)
    return tiles.transpose(0, 2, 1, 3, 4).reshape(groups, 3, (heads // groups) * tq, V7X_LANES)


def _bias_cmp(f, qpos, nb, tq, groups):
    dist = qpos[:, None] - (jnp.arange(nb)[None, :] * CMP_STRIDE + CMP_LEN - 1)
    bc = f[jnp.clip(dist, 0, 2 * V7X_LANES - 1)]
    heads = f.shape[1]
    nq = qpos.shape[0] // tq
    bc = bc.reshape(nq, tq, nb, groups, heads // groups).transpose(3, 0, 4, 1, 2)
    return bc.reshape(groups, nq, (heads // groups) * tq, nb)


def _rope_tables(pos):
    half = B_ROPE // 2
    inv = ROPE_THETA ** (-jnp.arange(half, dtype=F32) / half)
    ang = pos.astype(F32)[:, None] * inv[None, :]
    return jnp.cos(ang), jnp.sin(ang)


def _pad_cols(w):
    return jnp.pad(w, ((0, 0), (0, W_IN_PAD - w.shape[1])))


def _pad_rows(x, rows):
    return jnp.pad(x, ((0, 0), (0, rows - x.shape[1])) + ((0, 0),) * (x.ndim - 2))


def _gather_pages(pool, page_table):
    g = pool[page_table]
    return g.reshape((g.shape[0], g.shape[1] * g.shape[2], -1))


TK = 512


def _mixer_a(z, f_bias, *, decode, caches=None, page_table=None):
    b, tq_total = z.shape[0], z.shape[1]
    kcol = A_HEADS * HEAD_DIM
    k_new = z[:, :, kcol:kcol + A_KV_HEADS * HEAD_DIM]
    v_new = z[:, :, kcol + A_KV_HEADS * HEAD_DIM:kcol + 2 * A_KV_HEADS * HEAD_DIM]
    ki_new = z[:, :, TAIL_COL:TAIL_COL + IDX_DIM]
    hg = A_HEADS // A_KV_HEADS
    if not decode:
        tq = V7X_LANES
        t = tq_total
        n_chunks = t // TK
        mask = _dsa_select(z, z, tq=tq, tk=TK, n_keep=min(IDX_TOPK, t // 4), qpos0=0, kidx_col=TAIL_COL // V7X_LANES)
        bias = _bias_tiles(f_bias, tq, A_KV_HEADS)
        o = _fa(z, z, z, groups=A_KV_HEADS, hg=hg, dq=HEAD_DIM, dv=HEAD_DIM, tq=tq, tk=TK, nkc=n_chunks,
                n_chunks=n_chunks, scale=HEAD_DIM ** -0.5, mode="mask",
                k_col=lambda n: kcol // HEAD_DIM + n, v_col=lambda n: kcol // HEAD_DIM + A_KV_HEADS + n,
                mask=mask, bias=bias, name="dsa_attn")
    else:
        cache_k, cache_v, cache_idx = caches
        past = page_table.shape[1] * PAGE_SIZE
        tkeys = _round_up(past + ROW_TILE_DEC, TK)
        n_chunks = tkeys // TK
        z8 = _pad_rows(z, ROW_TILE_DEC)
        cat = lambda old, new: _pad_rows(jnp.concatenate([old, new], axis=1), tkeys)
        kidx = cat(_gather_pages(cache_idx, page_table), ki_new)
        k_all = cat(_gather_pages(cache_k, page_table), k_new)
        v_all = cat(_gather_pages(cache_v, page_table), v_new)
        tq = ROW_TILE_DEC
        mask = _dsa_select(z8, kidx, tq=tq, tk=TK, n_keep=min(IDX_TOPK, (past + tq_total) // 4), qpos0=past, kidx_col=None)
        bias = _bias_tiles(f_bias, tq, A_KV_HEADS)
        o = _fa(z8, k_all, v_all, groups=A_KV_HEADS, hg=hg, dq=HEAD_DIM, dv=HEAD_DIM, tq=tq, tk=TK, nkc=n_chunks,
                n_chunks=n_chunks, scale=HEAD_DIM ** -0.5, mode="mask", qpos0=past, mask=mask, bias=bias, name="dsa_attn_dec")
        o = o[:, :tq_total]
    return o.reshape(b * tq_total, A_HEADS * HEAD_DIM), (k_new, v_new, ki_new)


def _mixer_b(z, kv_norm, w_kvb, *, decode, caches=None, page_table=None):
    b, tq_total = z.shape[0], z.shape[1]
    w = w_kvb.reshape(B_LORA, B_HEADS, B_NOPE + B_V)
    wuk_t = w[..., :B_NOPE].transpose(1, 2, 0).astype(BF16)
    wuv = w[..., B_NOPE:].transpose(1, 0, 2).astype(BF16)
    scale = (B_NOPE + B_ROPE) ** -0.5
    if not decode:
        tq = V7X_LANES
        cos, sin = _rope_tables(jnp.arange(tq_total))
        qcat, ckr, c_new, kr_new = _mla_prep(z, kv_norm, wuk_t, cos, sin, tq=tq)
        n_chunks = tq_total // TK
        o = _fa(qcat, ckr, ckr, groups=1, hg=B_HEADS, dq=MLA_QK, dv=B_LORA, tq=tq, tk=TK, nkc=n_chunks, n_chunks=n_chunks,
                scale=scale, mode="causal", q_col=lambda n: 0, k_col=lambda n: 0, v_col=lambda n: 0, vproj=wuv, name="mla_attn")
    else:
        cache_lat, cache_rope = caches
        past = page_table.shape[1] * PAGE_SIZE
        tq = ROW_TILE_DEC
        z8 = _pad_rows(z, tq)
        cos, sin = _rope_tables(past + jnp.arange(tq))
        qcat, ckr8, c8, kr8 = _mla_prep(z8, kv_norm, wuk_t, cos, sin, tq=tq)
        c_new, kr_new = c8[:, :tq_total], kr8[:, :tq_total]
        tkeys = _round_up(past + tq, TK)
        n_chunks = tkeys // TK
        lat = _gather_pages(cache_lat, page_table).astype(BF16)
        rope = _gather_pages(cache_rope, page_table).astype(BF16)
        old = jnp.concatenate([lat, rope, jnp.zeros(lat.shape[:2] + (MLA_QK - B_LORA - B_ROPE,), BF16)], axis=-1)
        ckr = _pad_rows(jnp.concatenate([old, ckr8[:, :tq_total]], axis=1), tkeys)
        o = _fa(qcat, ckr, ckr, groups=1, hg=B_HEADS, dq=MLA_QK, dv=B_LORA, tq=tq, tk=TK, nkc=n_chunks, n_chunks=n_chunks,
                scale=scale, mode="causal", qpos0=past, q_col=lambda n: 0, k_col=lambda n: 0, v_col=lambda n: 0, vproj=wuv,
                name="mla_attn_dec")
        o = o[:, :tq_total]
    return o.reshape(b * tq_total, B_HEADS * B_V), (c_new, kr_new)


def _mixer_c(z, gate_b, wk_pos, wv_pos, f_bias, *, decode, caches=None, win=None, page_table=None):
    b, tq_total = z.shape[0], z.shape[1]
    gw = C_GROUPS * HEAD_DIM
    q0 = C_HEADS * HEAD_DIM
    kc, vc, ks, vs, kw, vw = (z[:, :, q0 + j * gw:q0 + (j + 1) * gw] for j in range(6))
    hg = C_HEADS // C_GROUPS
    scale = HEAD_DIM ** -0.5
    line = CMP_STRIDE * gw
    if not decode:
        t = tq_total
        tq = V7X_LANES
        n_sub = t // CMP_STRIDE
        kcb, vcb = _nsa_compress(kc.reshape(b, n_sub, line), vc.reshape(b, n_sub, line), wk_pos, wv_pos, n_out=n_sub)
        n_slc = -(-t // SLC_LEN)
        bias_c = _bias_cmp(f_bias, jnp.arange(t), n_sub, tq, C_GROUPS)
        o_cmp, mask = _nsa_cmp(z, kcb, vcb, bias_c, tq=tq, tk=TK, n_slc=n_slc, tkeys=t, qpos0=0)
        bias = _bias_tiles(f_bias, tq, C_GROUPS)
        n_chunks = t // TK
        blk = lambda j: (lambda n: (q0 + j * gw) // HEAD_DIM + n)
        o_slc = _fa(z, z, z, groups=C_GROUPS, hg=hg, dq=HEAD_DIM, dv=HEAD_DIM, tq=tq, tk=TK, nkc=n_chunks, n_chunks=n_chunks,
                    scale=scale, mode="mask", k_col=blk(2), v_col=blk(3), mask=mask, bias=bias, name="nsa_slc")
        tkw = V7X_LANES
        o_win = _fa(z, z, z, groups=C_GROUPS, hg=hg, dq=HEAD_DIM, dv=HEAD_DIM, tq=tq, tk=tkw, nkc=WINDOW // tkw + 1,
                    n_chunks=t // tkw, scale=scale, mode="window", k_col=blk(4), v_col=blk(5), bias=bias, name="nsa_win")
        n_w = min(WINDOW, t)
        win_new = (kw[:, t - n_w:], vw[:, t - n_w:])
        zrows = z.reshape(b * t, W_IN_PAD)
    else:
        cache_kc, cache_vc, cache_ks, cache_vs = caches
        win_k, win_v = win
        past = page_table.shape[1] * PAGE_SIZE
        tq = ROW_TILE_DEC
        z8 = _pad_rows(z, tq)
        n_sub = _round_up(-(-(past + tq_total) // CMP_STRIDE), V7X_SUBLANES)
        rows = n_sub * CMP_STRIDE
        cat = lambda old, new, r: _pad_rows(jnp.concatenate([old, new], axis=1), r)
        xk = cat(_gather_pages(cache_kc, page_table), kc, rows).reshape(b, n_sub, line)
        xv = cat(_gather_pages(cache_vc, page_table), vc, rows).reshape(b, n_sub, line)
        nb = _round_up(n_sub, V7X_LANES)
        kcb, vcb = _nsa_compress(xk, xv, wk_pos, wv_pos, n_out=nb)
        n_slc = -(-(past + tq_total) // SLC_LEN)
        tkeys = _round_up(n_slc * SLC_LEN, TK)
        bias_c = _bias_cmp(f_bias, past + jnp.arange(tq), nb, tq, C_GROUPS)
        o_cmp, mask = _nsa_cmp(z8, kcb, vcb, bias_c, tq=tq, tk=TK, n_slc=n_slc, tkeys=tkeys, qpos0=past)
        bias = _bias_tiles(f_bias, tq, C_GROUPS)
        ks_all = cat(_gather_pages(cache_ks, page_table), ks, tkeys)
        vs_all = cat(_gather_pages(cache_vs, page_table), vs, tkeys)
        n_chunks = tkeys // TK
        o_slc = _fa(z8, ks_all, vs_all, groups=C_GROUPS, hg=hg, dq=HEAD_DIM, dv=HEAD_DIM, tq=tq, tk=TK, nkc=n_chunks,
                    n_chunks=n_chunks, scale=scale, mode="mask", qpos0=past, mask=mask, bias=bias, name="nsa_slc_dec")
        n_buf = win_k.shape[1]
        tkw = V7X_LANES
        wrows = _round_up(n_buf + tq, tkw)
        kw_cat = jnp.concatenate([win_k.reshape(b, n_buf, gw), kw], axis=1)
        vw_cat = jnp.concatenate([win_v.reshape(b, n_buf, gw), vw], axis=1)
        o_win = _fa(z8, _pad_rows(kw_cat, wrows), _pad_rows(vw_cat, wrows), groups=C_GROUPS, hg=hg, dq=HEAD_DIM, dv=HEAD_DIM,
                    tq=tq, tk=tkw, nkc=wrows // tkw, n_chunks=wrows // tkw, scale=scale, mode="window", qpos0=past,
                    kpos0=past - n_buf, bias=bias, name="nsa_win_dec")
        n_w = min(WINDOW, n_buf + tq_total)
        win_new = (kw_cat[:, -n_w:], vw_cat[:, -n_w:])
        o_cmp, o_slc, o_win = (x[:, :tq_total] for x in (o_cmp, o_slc, o_win))
        zrows = z.reshape(b * tq_total, W_IN_PAD)
    hq = C_HEADS * HEAD_DIM
    o = _nsa_combine(o_cmp.reshape(-1, hq), o_slc.reshape(-1, hq), o_win.reshape(-1, hq), zrows, gate_b)
    return o, (kc, vc, ks, vs) + win_new


def kernel(x_prompt, x_sample, cache_a_k, cache_a_v, cache_a_idx, cache_b_latent, cache_b_rope, cache_c_cmp_k, cache_c_cmp_v, cache_c_slc_k, cache_c_slc_v, state_c_win_k, state_c_win_v, state_ffn_conv, page_table, rel_bias, attn_norm, ffn_norm, final_norm, a_w_in, a_w_o, b_w_in, b_kv_norm, b_w_kvb, b_w_o, c_w_in, c_gate_b, c_cmp_wk, c_cmp_wv, c_w_o, ffn_w_up, ffn_conv_w, ffn_conv_b, ffn_w_down):
    bp, t, d = x_prompt.shape
    bs, ts, _ = x_sample.shape
    hp = x_prompt.reshape(bp * t, d)
    hs = x_sample.reshape(bs * ts, d)
    f_bias = _bias_by_distance(rel_bias)
    flat = lambda pool: pool.reshape(pool.shape[:2] + (-1,))
    st_p = {"a": [], "b": [], "c": [], "conv": []}
    st_s = {"a": [], "b": [], "c": [], "conv": []}
    for i in range(DEPTH):
        kind, slot = i % 3, i // 3
        w_in = _pad_cols((a_w_in, b_w_in, c_w_in)[kind][slot])
        w_o = (a_w_o, b_w_o, c_w_o)[kind][slot]
        zp = _mm(hp, w_in, norm_g=attn_norm[i], tn=768, name="in_proj").reshape(bp, t, W_IN_PAD)
        zs = _mm(hs, w_in, norm_g=attn_norm[i], tn=768, name="in_proj_dec").reshape(bs, ts, W_IN_PAD)
        if kind == 0:
            op, sp = _mixer_a(zp, f_bias, decode=False)
            os_, ss = _mixer_a(zs, f_bias, decode=True,
                               caches=(flat(cache_a_k[slot]), flat(cache_a_v[slot]), cache_a_idx[slot]), page_table=page_table)
            key = "a"
        elif kind == 1:
            op, sp = _mixer_b(zp, b_kv_norm[slot], b_w_kvb[slot], decode=False)
            os_, ss = _mixer_b(zs, b_kv_norm[slot], b_w_kvb[slot], decode=True,
                               caches=(cache_b_latent[slot], cache_b_rope[slot]), page_table=page_table)
            key = "b"
        else:
            op, sp = _mixer_c(zp, c_gate_b[slot], c_cmp_wk[slot], c_cmp_wv[slot], f_bias, decode=False)
            os_, ss = _mixer_c(zs, c_gate_b[slot], c_cmp_wk[slot], c_cmp_wv[slot], f_bias, decode=True,
                               caches=tuple(flat(c[slot]) for c in (cache_c_cmp_k, cache_c_cmp_v, cache_c_slc_k, cache_c_slc_v)),
                               win=(state_c_win_k[slot], state_c_win_v[slot]), page_table=page_table)
            key = "c"
        st_p[key].append(sp)
        st_s[key].append(ss)
        hp = _mm(op, w_o, res=hp, name="out_proj")
        hs = _mm(os_, w_o, res=hs, name="out_proj_dec")
        fp, tail = _ffn_up(hp, ffn_norm[i], ffn_w_up[i], ffn_conv_w[i], ffn_conv_b[i], seq_len=t)
        fs, gs = _ffn_up(hs, ffn_norm[i], ffn_w_up[i], ffn_conv_w[i], ffn_conv_b[i], seq_len=ts, prev=state_ffn_conv[i])
        hp = _mm(fp, ffn_w_down[i], res=hp, name="ffn_down")
        hs = _mm(fs, ffn_w_down[i], res=hs, name="ffn_down_dec")
        blocks_per_seq = tail.shape[0] // bp
        st_p["conv"].append(tail[blocks_per_seq - 1::blocks_per_seq, V7X_SUBLANES - (CONV_W - 1):, :])
        st_s["conv"].append(gs.reshape(bs, ts, D_FF)[:, ts - (CONV_W - 1):])
    y_prompt = _rmsnorm(hp, final_norm).reshape(bp, t, d)
    y_sample = _rmsnorm(hs, final_norm).reshape(bs, ts, d)

    def stack(states, j, shape_tail):
        return jnp.stack([s[j].reshape(s[j].shape[:2] + shape_tail) for s in states])

    def group(st):
        kv = (A_KV_HEADS, HEAD_DIM)
        cg = (C_GROUPS, HEAD_DIM)
        return (stack(st["a"], 0, kv), stack(st["a"], 1, kv), stack(st["a"], 2, (IDX_DIM,)),
                stack(st["b"], 0, (B_LORA,)), stack(st["b"], 1, (B_ROPE,)),
                stack(st["c"], 0, cg), stack(st["c"], 1, cg), stack(st["c"], 2, cg), stack(st["c"], 3, cg),
                stack(st["c"], 4, cg), stack(st["c"], 5, cg), jnp.stack(st["conv"]))

    return (y_prompt, y_sample) + group(st_p) + group(st_s)
```

```python
import functools
import math

import jax
import jax.numpy as jnp
import numpy as np
from jax import lax
from jax.experimental import pallas as pl
from jax.experimental.pallas import tpu as pltpu

F32 = jnp.float32
BF16 = jnp.bfloat16
I32 = jnp.int32

D_MODEL = 2048
DEPTH = 4
PAGE_SIZE = 128
HEAD_DIM = 128
RMS_EPS = 1e-6
NEG = -1e30
FORCE_SCORE = 1e9
N_BUCKETS = 32
MAX_DISTANCE = 128
A_HEADS = 16
A_KV_HEADS = 2
IDX_HEADS = 16
IDX_DIM = 64
IDX_TOPK = 256
B_HEADS = 16
B_NOPE = 128
B_ROPE = 64
B_V = 128
B_LORA = 512
ROPE_THETA = 10000.0
C_HEADS = 16
C_GROUPS = 2
CMP_LEN = 32
CMP_STRIDE = 16
SLC_LEN = 64
SLC_TOPN = 16
WINDOW = 512
D_FF = 4096
CONV_W = 3

V7X_LANES = 128
V7X_SUBLANES = 8
V7X_VMEM_BYTES = 64 * 1024 * 1024
VMEM_LIMIT_BYTES = V7X_VMEM_BYTES - 8 * 1024 * 1024

W_IN_PAD = 3840
TAIL_COL = 3584
ROW_TILE_DEC = 8
MLA_QK = 640
BIAS_SAT = 113
MINF = -3.0e38


def _cparams(sem):
    return pltpu.CompilerParams(dimension_semantics=sem, vmem_limit_bytes=VMEM_LIMIT_BYTES)


def _round_up(x, m):
    return (x + m - 1) // m * m


def _mm_kernel(*refs, has_norm, has_res, cast_a):
    it = iter(refs)
    a_ref = next(it)
    g_ref = next(it) if has_norm else None
    w_ref = next(it)
    r_ref = next(it) if has_res else None
    o_ref = next(it)
    abf_ref = next(it) if cast_a else None
    if cast_a:
        @pl.when(pl.program_id(1) == 0)
        def _():
            a = a_ref[...].astype(F32)
            if has_norm:
                a = a * lax.rsqrt(jnp.mean(a * a, axis=-1, keepdims=True) + RMS_EPS) * g_ref[...]
            abf_ref[...] = a.astype(BF16)
        a = abf_ref[...]
    else:
        a = a_ref[...]
    acc = jnp.dot(a, w_ref[...].astype(BF16), preferred_element_type=F32)
    if has_res:
        acc = acc + r_ref[...]
    o_ref[...] = acc.astype(o_ref.dtype)


def _mm(a, w, *, norm_g=None, res=None, tn=512, name="mm"):
    m, k = a.shape
    n = w.shape[1]
    tm = min(m, 1024)
    assert m % tm == 0 and n % tn == 0
    has_norm, has_res = norm_g is not None, res is not None
    cast_a = has_norm or a.dtype != BF16
    ins = [a]
    specs = [pl.BlockSpec((tm, k), lambda i, j: (i, 0))]
    if has_norm:
        ins.append(norm_g.reshape(1, k))
        specs.append(pl.BlockSpec((1, k), lambda i, j: (0, 0)))
    ins.append(w)
    specs.append(pl.BlockSpec((k, tn), lambda i, j: (0, j)))
    if has_res:
        ins.append(res)
        specs.append(pl.BlockSpec((tm, tn), lambda i, j: (i, j)))
    return pl.pallas_call(
        functools.partial(_mm_kernel, has_norm=has_norm, has_res=has_res, cast_a=cast_a),
        out_shape=jax.ShapeDtypeStruct((m, n), F32),
        grid=(m // tm, n // tn),
        in_specs=specs,
        out_specs=pl.BlockSpec((tm, tn), lambda i, j: (i, j)),
        scratch_shapes=[pltpu.VMEM((tm, k), BF16)] if cast_a else [],
        compiler_params=_cparams(("parallel", "arbitrary")),
        name=name,
    )(*ins)


def _rmsnorm_kernel(x_ref, g_ref, o_ref):
    x = x_ref[...]
    o_ref[...] = x * lax.rsqrt(jnp.mean(x * x, axis=-1, keepdims=True) + RMS_EPS) * g_ref[...]


def _rmsnorm(x, g):
    m, d = x.shape
    tm = min(m, 1024)
    return pl.pallas_call(
        _rmsnorm_kernel,
        out_shape=jax.ShapeDtypeStruct((m, d), F32),
        grid=(m // tm,),
        in_specs=[pl.BlockSpec((tm, d), lambda i: (i, 0)), pl.BlockSpec((1, d), lambda i: (0, 0))],
        out_specs=pl.BlockSpec((tm, d), lambda i: (i, 0)),
        compiler_params=_cparams(("parallel",)),
        name="final_norm",
    )(x, g.reshape(1, d))


def _ffn_up_kernel(*refs, tm, blocks_per_seq, decode):
    if decode:
        x_ref, gn_ref, wg_ref, wu_ref, cw_ref, cb_ref, p1_ref, p2_ref, h_ref, g_ref, xn_ref = refs
    else:
        x_ref, gn_ref, wg_ref, wu_ref, cw_ref, cb_ref, h_ref, tail_ref, xn_ref, carry_ref = refs
    i = pl.program_id(0)
    j = pl.program_id(1)

    @pl.when(j == 0)
    def _():
        x = x_ref[...]
        xn_ref[...] = (x * lax.rsqrt(jnp.mean(x * x, axis=-1, keepdims=True) + RMS_EPS) * gn_ref[...]).astype(BF16)

    xn = xn_ref[...]
    g = jnp.dot(xn, wg_ref[...].astype(BF16), preferred_element_type=F32)
    u = jnp.dot(xn, wu_ref[...].astype(BF16), preferred_element_type=F32)
    row = lax.broadcasted_iota(I32, g.shape, 0)
    r1 = pltpu.roll(g, 1, axis=0)
    r2 = pltpu.roll(g, 2, axis=0)
    if decode:
        t = row & 3
        gm1 = jnp.where(t >= 1, r1, p1_ref[...])
        gm2 = jnp.where(t >= 2, r2, p2_ref[...])
        g_ref[...] = g
    else:
        tf = g.shape[1]
        @pl.when(i == 0)
        def _():
            carry_ref[j] = jnp.zeros((V7X_SUBLANES, tf), F32)

        c = jnp.where(i % blocks_per_seq == 0, 0.0, carry_ref[j])
        c1 = c[V7X_SUBLANES - 1:V7X_SUBLANES, :]
        c2 = c[V7X_SUBLANES - 2:V7X_SUBLANES - 1, :]
        gm1 = jnp.where(row == 0, c1, r1)
        gm2 = jnp.where(row == 0, c2, jnp.where(row == 1, c1, r2))
        tail = g[tm - V7X_SUBLANES:, :]
        carry_ref[j] = tail
        tail_ref[0] = tail
    cw = cw_ref[...]
    conv = cb_ref[...] + gm2 * cw[0:1, :] + gm1 * cw[1:2, :] + g * cw[2:3, :]
    h = conv * (1.0 / (1.0 + jnp.exp(-conv))) * u
    h_ref[...] = h.astype(BF16)


def _ffn_up(x, gn, w_up, conv_w, conv_b, *, seq_len, prev=None, tf=512):
    m, d = x.shape
    nf = D_FF // tf
    decode = prev is not None
    tm = min(m, 1024)
    ins = [x, gn.reshape(1, d), w_up, w_up, conv_w, conv_b.reshape(1, D_FF)]
    specs = [
        pl.BlockSpec((tm, d), lambda i, j: (i, 0)),
        pl.BlockSpec((1, d), lambda i, j: (0, 0)),
        pl.BlockSpec((d, tf), lambda i, j: (0, j)),
        pl.BlockSpec((d, tf), lambda i, j: (0, j + nf)),
        pl.BlockSpec((CONV_W, tf), lambda i, j: (0, j)),
        pl.BlockSpec((1, tf), lambda i, j: (0, j)),
    ]
    if decode:
        assert seq_len == 4 and m == tm
        zero = jnp.zeros_like(prev[:, :1])
        p1 = jnp.concatenate([prev[:, 1:2], zero, zero, zero], axis=1).reshape(m, D_FF)
        p2 = jnp.concatenate([prev[:, 0:1], prev[:, 1:2], zero, zero], axis=1).reshape(m, D_FF)
        ins += [p1, p2]
        specs += [pl.BlockSpec((tm, tf), lambda i, j: (i, j))] * 2
        out_shape = (jax.ShapeDtypeStruct((m, D_FF), BF16), jax.ShapeDtypeStruct((m, D_FF), F32))
        out_specs = (pl.BlockSpec((tm, tf), lambda i, j: (i, j)), pl.BlockSpec((tm, tf), lambda i, j: (i, j)))
        scratch = [pltpu.VMEM((tm, d), BF16)]
        bps = 1
    else:
        assert seq_len % tm == 0
        bps = seq_len // tm
        out_shape = (jax.ShapeDtypeStruct((m, D_FF), BF16), jax.ShapeDtypeStruct((m // tm, V7X_SUBLANES, D_FF), F32))
        out_specs = (pl.BlockSpec((tm, tf), lambda i, j: (i, j)), pl.BlockSpec((1, V7X_SUBLANES, tf), lambda i, j: (i, 0, j)))
        scratch = [pltpu.VMEM((tm, d), BF16), pltpu.VMEM((nf, V7X_SUBLANES, tf), F32)]
    return pl.pallas_call(
        functools.partial(_ffn_up_kernel, tm=tm, blocks_per_seq=bps, decode=decode),
        out_shape=out_shape,
        grid=(m // tm, nf),
        in_specs=specs,
        out_specs=out_specs,
        scratch_shapes=scratch,
        compiler_params=_cparams(("arbitrary", "arbitrary")),
        name="ffn_up_dec" if decode else "ffn_up",
    )(*ins)


def _fa_chunk(i, kc, *, mode, tq, tk, qpos0, kpos0, n_chunks):
    if mode == "window":
        first = (qpos0 + i * tq - WINDOW + 1 - kpos0) // tk
        return first + kc
    return kc


def _fa_chunk_clamped(i, kc, **kw):
    c = _fa_chunk(i, kc, **kw)
    if kw["mode"] != "window":
        last = (kw["qpos0"] + i * kw["tq"] + kw["tq"] - 1 - kw["kpos0"]) // kw["tk"]
        c = jnp.minimum(c, last)
    return jnp.clip(c, 0, kw["n_chunks"] - 1)


def _fa_kernel(*refs, hg, dq, dv, tq, tk, nkc, scale, mode, qpos0, kpos0, n_chunks,
               has_mask, has_bias, has_vproj):
    it = iter(refs)
    q_ref, k_ref, v_ref = next(it), next(it), next(it)
    mask_ref = next(it) if has_mask else None
    bias_ref = next(it) if has_bias else None
    wv_ref = next(it) if has_vproj else None
    o_ref, qs_ref, m_ref, acc_ref = next(it), next(it), next(it), next(it)
    i = pl.program_id(1)
    n = pl.program_id(2)
    kc = pl.program_id(3)
    t0 = qpos0 + i * tq
    geo = dict(mode=mode, tq=tq, tk=tk, qpos0=qpos0, kpos0=kpos0, n_chunks=n_chunks)

    @pl.when(kc == 0)
    def _():
        for h in range(hg):
            qs_ref[h * tq:(h + 1) * tq, :] = q_ref[0, :, h * dq:(h + 1) * dq].astype(BF16)
        m_ref[...] = jnp.full(m_ref.shape, NEG, F32)
        acc_ref[...] = jnp.zeros(acc_ref.shape, F32)

    craw = _fa_chunk(i, kc, **geo)
    s0 = kpos0 + craw * tk
    needed = (craw >= 0) & (craw < n_chunks) & (s0 <= t0 + tq - 1)

    @pl.when(needed)
    def _():
        kb = k_ref[0].astype(BF16)
        s = lax.dot_general(qs_ref[...], kb, (((1,), (1,)), ((), ())), preferred_element_type=F32) * scale
        if has_bias:
            parts = []
            for u in range(tk // V7X_LANES):
                dsel = jnp.clip(t0 // V7X_LANES - (s0 // V7X_LANES + u), 0, 2)
                parts.append(bias_ref[n, dsel])
            s = s + (parts[0] if len(parts) == 1 else jnp.concatenate(parts, axis=1))
        add = None
        if has_mask:
            add = mask_ref[0, 0, 0]
        if mode in ("causal", "window"):
            qp = t0 + lax.broadcasted_iota(I32, (tq, tk), 0)
            kp = s0 + lax.broadcasted_iota(I32, (tq, tk), 1)
            ok = kp <= qp
            if mode == "window":
                ok = ok & (kp > qp - WINDOW) & (kp >= 0)
            add = jnp.where(ok, 0.0, NEG)
        if add is not None:
            s = s + (add if hg == 1 else jnp.concatenate([add] * hg, axis=0))
        m_old = m_ref[...]
        m_new = jnp.maximum(m_old, jnp.max(s, axis=1, keepdims=True))
        alpha = jnp.exp(m_old - m_new)
        p = jnp.exp(s - jnp.concatenate([m_new] * (tk // V7X_LANES), axis=1))
        vb = v_ref[0].astype(BF16)
        vext = jnp.concatenate([vb, jnp.ones((tk, V7X_LANES), BF16)], axis=1)
        pv = jnp.dot(p.astype(BF16), vext, preferred_element_type=F32)
        acc_ref[...] = acc_ref[...] * jnp.concatenate([alpha] * (dv // V7X_LANES + 1), axis=1) + pv
        m_ref[...] = m_new

    @pl.when(kc == nkc - 1)
    def _():
        valid = m_ref[...] > 0.5 * NEG
        l = jnp.where(valid, acc_ref[:, dv:dv + V7X_LANES], 1.0)
        inv = jnp.where(valid, 1.0 / l, 0.0)
        o = acc_ref[:, :dv] * jnp.concatenate([inv] * (dv // V7X_LANES), axis=1)
        for h in range(hg):
            oh = o[h * tq:(h + 1) * tq, :]
            if has_vproj:
                oh = jnp.dot(oh.astype(BF16), wv_ref[h], preferred_element_type=F32)
            o_ref[0, :, h * HEAD_DIM:(h + 1) * HEAD_DIM] = oh


def _fa(q, k, v, *, groups, hg, dq, dv, tq, tk, nkc, n_chunks, scale, mode, qpos0=0, kpos0=0,
        q_col=lambda n: n, k_col=lambda n: n, v_col=lambda n: n, mask=None, bias=None, vproj=None, name="fa"):
    b, tq_total = q.shape[0], q.shape[1]
    nq = tq_total // tq
    geo = dict(mode=mode, tq=tq, tk=tk, qpos0=qpos0, kpos0=kpos0, n_chunks=n_chunks)
    kmap = lambda col: (lambda bb, i, n, kc: (bb, _fa_chunk_clamped(i, kc, **geo), col(n)))
    ins = [q, k, v]
    specs = [
        pl.BlockSpec((1, tq, hg * dq), lambda bb, i, n, kc: (bb, i, q_col(n))),
        pl.BlockSpec((1, tk, dq), kmap(k_col)),
        pl.BlockSpec((1, tk, dv), kmap(v_col)),
    ]
    if mask is not None:
        ins.append(mask)
        gsel = (lambda n: n) if mask.shape[1] > 1 else (lambda n: 0)
        specs.append(pl.BlockSpec((1, 1, 1, tq, tk), lambda bb, i, n, kc: (bb, gsel(n), _fa_chunk_clamped(i, kc, **geo), i, 0)))
    if bias is not None:
        ins.append(bias)
        specs.append(pl.BlockSpec(bias.shape, lambda bb, i, n, kc: (0, 0, 0, 0)))
    if vproj is not None:
        ins.append(vproj)
        specs.append(pl.BlockSpec(vproj.shape, lambda bb, i, n, kc: (0, 0, 0)))
    r = hg * tq
    return pl.pallas_call(
        functools.partial(_fa_kernel, hg=hg, dq=dq, dv=dv, tq=tq, tk=tk, nkc=nkc, scale=scale, mode=mode,
                          qpos0=qpos0, kpos0=kpos0, n_chunks=n_chunks, has_mask=mask is not None,
                          has_bias=bias is not None, has_vproj=vproj is not None),
        out_shape=jax.ShapeDtypeStruct((b, tq_total, groups * hg * HEAD_DIM), F32),
        grid=(b, nq, groups, nkc),
        in_specs=specs,
        out_specs=pl.BlockSpec((1, tq, hg * HEAD_DIM), lambda bb, i, n, kc: (bb, i, n)),
        scratch_shapes=[pltpu.VMEM((r, dq), BF16), pltpu.VMEM((r, V7X_LANES), F32), pltpu.VMEM((r, dv + V7X_LANES), F32)],
        compiler_params=_cparams(("parallel", "parallel", "arbitrary", "arbitrary")),
        name=name,
    )(*ins)


def _sortable(x):
    b = lax.bitcast_convert_type(x, I32)
    return b ^ ((b >> 31) & jnp.int32(0x7FFFFFFF))


_NT = (((1,), (1,)), ((), ()))


def _select_scores(qstack, kic, wcols, rows):
    d = lax.dot_general(qstack, kic, _NT, preferred_element_type=F32)
    acc = None
    for h in range(IDX_HEADS):
        term = jnp.maximum(d[h * rows:(h + 1) * rows], 0.0) * wcols[h]
        acc = term if acc is None else acc + term
    return acc + 0.0


def _select_topk(key_ref, emit, *, rows, tk, nck, n_chunks, n_keep, qpos, live):
    lane = lax.broadcasted_iota(I32, (rows, tk), 1)
    nl = tk // V7X_LANES

    def count(pred):
        def body(kc, c):
            hit = jnp.where(pred(key_ref[kc], kc), 1.0, 0.0)
            part = hit[:, :V7X_LANES]
            for u in range(1, nl):
                part = part + hit[:, u * V7X_LANES:(u + 1) * V7X_LANES]
            return c + part
        return jnp.sum(lax.fori_loop(0, nck, body, jnp.zeros((rows, V7X_LANES), F32)), axis=1, keepdims=True)

    neg_key = jnp.int32(np.array(NEG, np.float32).view(np.int32) ^ 0x7FFFFFFF)
    extra = jnp.asarray((n_chunks - nck) * tk).astype(F32)
    kf = float(n_keep)

    def count_ge(cand):
        return count(lambda kk, kc: kk >= cand) + jnp.where(cand <= neg_key, extra, 0.0)

    tau = jnp.where(count_ge(jnp.zeros((rows, 1), I32)) >= kf, 0, jnp.iinfo(jnp.int32).min).astype(I32)
    for bit in range(30, -1, -1):
        cand = tau | jnp.int32(1 << bit)
        tau = jnp.where(count_ge(cand) >= kf, cand, tau)
    n_ge = count_ge(tau)
    ties = jnp.max(jnp.where((n_ge != kf) & live, 1.0, 0.0)) > 0.5

    def write(sel_fn):
        def body(kc, carry):
            ok = sel_fn(key_ref[kc], kc) & (kc * tk + lane <= qpos)
            emit(kc, jnp.where(ok, 0.0, NEG))
            return carry
        lax.fori_loop(0, nck, body, 0)

        def fill(kc, carry):
            emit(kc, jnp.full((rows, tk), NEG, F32))
            return carry
        lax.fori_loop(nck, n_chunks, fill, 0)

    @pl.when(jnp.logical_not(ties))
    def _():
        write(lambda kk, kc: kk >= tau)

    @pl.when(ties)
    def _():
        r = kf - count(lambda kk, kc: kk > tau) - jnp.where(tau < neg_key, extra, 0.0)
        nbits = int(math.ceil(math.log2(n_chunks * tk)))
        mth = jnp.zeros((rows, 1), I32)
        for bit in range(nbits - 1, -1, -1):
            cand = mth | jnp.int32(1 << bit)
            c = count(lambda kk, kc: (kk == tau) & (kc * tk + lane < cand))
            c = c + jnp.where(tau == neg_key, jnp.clip(cand - nck * tk, 0, (n_chunks - nck) * tk).astype(F32), 0.0)
            mth = jnp.where(c < r, cand, mth)
        write(lambda kk, kc: (kk > tau) | ((kk == tau) & (kc * tk + lane <= mth)))


def _stack_index_queries(qs_ref, qa, qb, rows):
    hh = IDX_HEADS // 2
    for h in range(IDX_HEADS):
        src = qa if h < hh else qb
        qs_ref[h * rows:(h + 1) * rows, :] = src[:, (h % hh) * IDX_DIM:(h % hh + 1) * IDX_DIM].astype(BF16)


def _dsa_select_kernel(qa_ref, qb_ref, wi_ref, kidx_ref, o_ref, key_ref, qs_ref, *, tq, tk, n_chunks, n_keep):
    i = pl.program_id(1)
    t0 = i * tq
    nck = jnp.minimum((t0 + tq - 1) // tk + 1, n_chunks)
    _stack_index_queries(qs_ref, qa_ref[0], qb_ref[0], tq)
    w = wi_ref[0][:, IDX_DIM:IDX_DIM + IDX_HEADS] * (IDX_HEADS ** -0.5)
    wcols = [w[:, h:h + 1] for h in range(IDX_HEADS)]
    qpos = t0 + lax.broadcasted_iota(I32, (tq, tk), 0)
    lane = lax.broadcasted_iota(I32, (tq, tk), 1)

    def score_chunk(kc, carry):
        kic = kidx_ref[0, pl.ds(pl.multiple_of(kc * tk, tk), tk), :][:, :IDX_DIM].astype(BF16)
        sc = _select_scores(qs_ref[...], kic, wcols, tq)
        key_ref[kc] = _sortable(jnp.where(kc * tk + lane <= qpos, sc, NEG))
        return carry

    lax.fori_loop(0, nck, score_chunk, 0)

    def emit(kc, mask):
        o_ref[0, 0, kc] = mask

    _select_topk(key_ref, emit, rows=tq, tk=tk, nck=nck, n_chunks=n_chunks, n_keep=n_keep, qpos=qpos,
                 live=jnp.full((tq, 1), True))


def _dsa_select(z, *, tq, tk, n_keep):
    b, t = z.shape[0], z.shape[1]
    n_chunks = t // tk
    half = IDX_HEADS * IDX_DIM // 2
    qi_col = (A_HEADS * HEAD_DIM + 2 * A_KV_HEADS * HEAD_DIM) // half
    tail = TAIL_COL // V7X_LANES
    return pl.pallas_call(
        functools.partial(_dsa_select_kernel, tq=tq, tk=tk, n_chunks=n_chunks, n_keep=n_keep),
        out_shape=jax.ShapeDtypeStruct((b, 1, n_chunks, t, tk), F32),
        grid=(b, t // tq),
        in_specs=[
            pl.BlockSpec((1, tq, half), lambda bb, i: (bb, i, qi_col)),
            pl.BlockSpec((1, tq, half), lambda bb, i: (bb, i, qi_col + 1)),
            pl.BlockSpec((1, tq, V7X_LANES), lambda bb, i: (bb, i, tail)),
            pl.BlockSpec((1, t, V7X_LANES), lambda bb, i: (bb, 0, tail)),
        ],
        out_specs=pl.BlockSpec((1, 1, n_chunks, tq, tk), lambda bb, i: (bb, 0, 0, i, 0)),
        scratch_shapes=[pltpu.VMEM((n_chunks, tq, tk), I32), pltpu.VMEM((IDX_HEADS * tq, IDX_DIM), BF16)],
        compiler_params=_cparams(("parallel", "arbitrary")),
        name="dsa_select",
    )(z, z, z, z)


def _page_copies(pt_ref, pool_ref, dst, sem, *, slot, row, n_pages, rows_per_page):
    def at(p):
        return pltpu.make_async_copy(
            pool_ref.at[slot, pt_ref[row, p]],
            dst.at[pl.ds(pl.multiple_of(p * rows_per_page, rows_per_page), rows_per_page)], sem)
    return at


def _dsa_select_paged_kernel(pt_ref, qa_ref, qb_ref, wi_ref, knew_ref, pool_ref, o_ref, kbuf, sem, key_ref, qs_ref, *,
                             slot, tb, tk, n_pages, n_keep, past, n_steps):
    s = pl.program_id(0)
    tq = ROW_TILE_DEC
    rows = tb * tq
    nc_past = past // tk
    n_chunks = nc_past + 1

    def each_page(step, si, fn):
        for j in range(tb):
            cp = _page_copies(pt_ref, pool_ref, kbuf.at[si, j], sem.at[si], slot=slot, row=step * tb + j,
                              n_pages=n_pages, rows_per_page=PAGE_SIZE)

            def body(p, carry):
                fn(cp(p))
                return carry
            lax.fori_loop(0, n_pages, body, 0)

    @pl.when(s == 0)
    def _():
        each_page(0, 0, lambda c: c.start())

    @pl.when(s + 1 < n_steps)
    def _():
        each_page(s + 1, (s + 1) % 2, lambda c: c.start())

    si = s % 2
    each_page(s, si, lambda c: c.wait())

    lane_new = lax.broadcasted_iota(I32, (tq, PAGE_SIZE), 1)
    arow = lax.broadcasted_iota(I32, (tq, PAGE_SIZE), 0)
    for j in range(tb):
        _stack_index_queries(qs_ref, qa_ref[j], qb_ref[j], tq)
        w = wi_ref[j][:, IDX_DIM:IDX_DIM + IDX_HEADS] * (IDX_HEADS ** -0.5)
        wcols = [w[:, h:h + 1] for h in range(IDX_HEADS)]

        def score_chunk(kc, carry):
            kic = kbuf[si, j, pl.ds(pl.multiple_of(kc * tk, tk), tk), :].astype(BF16)
            key_ref[kc, j * tq:(j + 1) * tq, :] = _sortable(_select_scores(qs_ref[...], kic, wcols, tq))
            return carry

        lax.fori_loop(0, nc_past, score_chunk, 0)
        sc = _select_scores(qs_ref[...], knew_ref[j].astype(BF16), wcols, tq)
        sc = jnp.where(lane_new <= arow, sc, NEG)
        key_ref[nc_past, j * tq:(j + 1) * tq, :] = _sortable(
            jnp.concatenate([sc, jnp.full((tq, tk - PAGE_SIZE), NEG, F32)], axis=1))

    a = lax.broadcasted_iota(I32, (rows, tk), 0) % tq
    qpos = past + a

    def emit(kc, mask):
        for j in range(tb):
            o_ref[j, 0, kc] = mask[j * tq:(j + 1) * tq]

    live = lax.broadcasted_iota(I32, (rows, 1), 0) % tq < tq // 2
    _select_topk(key_ref, emit, rows=rows, tk=tk, nck=n_chunks, n_chunks=n_chunks, n_keep=n_keep, qpos=qpos, live=live)


def _dsa_select_paged(z8, knew, pool, page_table, *, slot, tb, tk, n_keep):
    b = z8.shape[0]
    n_pages = page_table.shape[1]
    past = n_pages * PAGE_SIZE
    n_chunks = past // tk + 1
    tq = ROW_TILE_DEC
    half = IDX_HEADS * IDX_DIM // 2
    qi_col = (A_HEADS * HEAD_DIM + 2 * A_KV_HEADS * HEAD_DIM) // half
    tail = TAIL_COL // V7X_LANES
    n_steps = b // tb
    grid_spec = pltpu.PrefetchScalarGridSpec(
        num_scalar_prefetch=1,
        grid=(n_steps,),
        in_specs=[
            pl.BlockSpec((tb, tq, half), lambda s, pt: (s, 0, qi_col)),
            pl.BlockSpec((tb, tq, half), lambda s, pt: (s, 0, qi_col + 1)),
            pl.BlockSpec((tb, tq, V7X_LANES), lambda s, pt: (s, 0, tail)),
            pl.BlockSpec((tb, PAGE_SIZE, IDX_DIM), lambda s, pt: (s, 0, 0)),
            pl.BlockSpec(memory_space=pl.ANY),
        ],
        out_specs=pl.BlockSpec((tb, 1, n_chunks, tq, tk), lambda s, pt: (s, 0, 0, 0, 0)),
        scratch_shapes=[
            pltpu.VMEM((2, tb, past, IDX_DIM), F32),
            pltpu.SemaphoreType.DMA((2,)),
            pltpu.VMEM((n_chunks, tb * tq, tk), I32),
            pltpu.VMEM((IDX_HEADS * tq, IDX_DIM), BF16),
        ],
    )
    return pl.pallas_call(
        functools.partial(_dsa_select_paged_kernel, slot=slot, tb=tb, tk=tk, n_pages=n_pages, n_keep=n_keep, past=past,
                          n_steps=n_steps),
        out_shape=jax.ShapeDtypeStruct((b, 1, n_chunks, tq, tk), F32),
        grid_spec=grid_spec,
        compiler_params=_cparams(("arbitrary",)),
        name="dsa_select_dec",
    )(page_table, z8, z8, z8, knew, pool)


def _paged_attn_kernel(pt_ref, q_ref, knew_ref, vnew_ref, pk_ref, pv_ref, *rest, kind, slot, nc, nb, depth, past, scale,
                       mask_groups, has_bias, has_vproj):
    it = iter(rest)
    mask_ref = next(it) if mask_groups else None
    bias_ref = next(it) if has_bias else None
    wv_ref = next(it) if has_vproj else None
    o_ref, kbuf, vbuf, sem, qs_ref, m_ref, acc_ref = (next(it) for _ in range(7))
    b = pl.program_id(0)
    tq = ROW_TILE_DEC
    tk = kbuf.shape[1]
    ppc = tk // PAGE_SIZE
    groups, hg = (A_KV_HEADS, A_HEADS // A_KV_HEADS) if kind == "gqa" else (1, B_HEADS)
    dv = HEAD_DIM if kind == "gqa" else B_LORA
    total = nb * nc

    def copies(g):
        bb = g // nc
        cc = g - bb * nc
        si = g % depth
        out = []
        for u in range(ppc):
            page = pt_ref[bb, cc * ppc + u]
            rows = pl.ds(u * PAGE_SIZE, PAGE_SIZE)
            out.append(pltpu.make_async_copy(pk_ref.at[slot, page], kbuf.at[si, rows], sem.at[si, 0]))
            out.append(pltpu.make_async_copy(pv_ref.at[slot, page], vbuf.at[si, rows], sem.at[si, 1]))
        return out

    @pl.when(b == 0)
    def _():
        for g in range(depth - 1):
            for c in copies(g):
                c.start()

    for n in range(groups):
        for h in range(hg):
            if kind == "gqa":
                qs_ref[n, h * tq:(h + 1) * tq, :] = q_ref[0, :, (n * hg + h) * HEAD_DIM:(n * hg + h + 1) * HEAD_DIM].astype(BF16)
            else:
                qs_ref[n, h * tq:(h + 1) * tq, :] = q_ref[0, :, h * MLA_QK:(h + 1) * MLA_QK]
    m_ref[...] = jnp.full(m_ref.shape, NEG, F32)
    acc_ref[...] = jnp.zeros(acc_ref.shape, F32)

    def update(n, s, vb):
        keys = s.shape[1]
        m_old = m_ref[n]
        m_new = jnp.maximum(m_old, jnp.max(s, axis=1, keepdims=True))
        alpha = jnp.exp(m_old - m_new)
        p = jnp.exp(s - jnp.concatenate([m_new] * (keys // V7X_LANES), axis=1))
        vext = jnp.concatenate([vb, jnp.ones((keys, V7X_LANES), BF16)], axis=1)
        pv = jnp.dot(p.astype(BF16), vext, preferred_element_type=F32)
        acc_ref[n] = acc_ref[n] * jnp.concatenate([alpha] * (dv // V7X_LANES + 1), axis=1) + pv
        m_ref[n] = m_new

    def attend(kb, vb, add, bias_sel):
        kb = kb.astype(BF16)
        vb = vb.astype(BF16)
        for n in range(groups):
            if kind == "gqa":
                s = lax.dot_general(qs_ref[n], kb[:, n * HEAD_DIM:(n + 1) * HEAD_DIM], _NT, preferred_element_type=F32)
                vn = vb[:, n * HEAD_DIM:(n + 1) * HEAD_DIM]
            else:
                q = qs_ref[n]
                s = (lax.dot_general(q[:, :B_LORA], kb, _NT, preferred_element_type=F32)
                     + lax.dot_general(q[:, B_LORA:B_LORA + B_ROPE], vb, _NT, preferred_element_type=F32))
                vn = kb
            s = s * scale
            if has_bias:
                parts = [bias_ref[n, d] for d in bias_sel]
                s = s + (parts[0] if len(parts) == 1 else jnp.concatenate(parts, axis=1))
            a = add(n)
            if a is not None:
                s = s + jnp.concatenate([a] * hg, axis=0)
            update(n, s, vn)

    def chunk(c, carry):
        g = b * nc + c
        nxt = g + depth - 1

        @pl.when(nxt < total)
        def _():
            for cp in copies(nxt):
                cp.start()

        for cp in copies(g):
            cp.wait()
        si = g % depth
        qtile = past // V7X_LANES
        bias_sel = [jnp.clip(qtile - (c * ppc + u), 0, 2) for u in range(ppc)]
        add = (lambda n: mask_ref[0, n if mask_groups > 1 else 0, c]) if mask_groups else (lambda n: None)
        attend(kbuf[si], vbuf[si], add, bias_sel)
        return carry

    lax.fori_loop(0, nc, chunk, 0)

    if mask_groups:
        add_new = lambda n: mask_ref[0, n if mask_groups > 1 else 0, nc][:, :PAGE_SIZE]
    else:
        kp = lax.broadcasted_iota(I32, (tq, PAGE_SIZE), 1)
        qp = lax.broadcasted_iota(I32, (tq, PAGE_SIZE), 0)
        causal = jnp.where(kp <= qp, 0.0, NEG)
        add_new = lambda n: causal
    attend(knew_ref[0], vnew_ref[0], add_new, [0])

    for n in range(groups):
        valid = m_ref[n] > 0.5 * NEG
        acc = acc_ref[n]
        l = jnp.where(valid, acc[:, dv:dv + V7X_LANES], 1.0)
        inv = jnp.where(valid, 1.0 / l, 0.0)
        o = acc[:, :dv] * jnp.concatenate([inv] * (dv // V7X_LANES), axis=1)
        for h in range(hg):
            oh = o[h * tq:(h + 1) * tq, :]
            if has_vproj:
                oh = jnp.dot(oh.astype(BF16), wv_ref[h], preferred_element_type=F32)
            o_ref[0, :, (n * hg + h) * HEAD_DIM:(n * hg + h + 1) * HEAD_DIM] = oh


def _paged_attn(q, knew, vnew, pool_k, pool_v, page_table, *, kind, slot, scale, tk=512, depth=8, mask=None, bias=None, vproj=None,
                name="paged_attn"):
    b = q.shape[0]
    n_pages = page_table.shape[1]
    nc = n_pages * PAGE_SIZE // tk
    past = n_pages * PAGE_SIZE
    assert b * nc >= depth
    tq = ROW_TILE_DEC
    if kind == "gqa":
        groups, hg, dq, dv, qw = A_KV_HEADS, A_HEADS // A_KV_HEADS, HEAD_DIM, HEAD_DIM, A_HEADS * HEAD_DIM
    else:
        groups, hg, dq, dv, qw = 1, B_HEADS, MLA_QK, B_LORA, B_HEADS * MLA_QK
    wk, wv = pool_k.shape[-1], pool_v.shape[-1]
    ins = [page_table, q, knew, vnew, pool_k, pool_v]
    specs = [
        pl.BlockSpec((1, tq, qw), lambda i, pt: (i, 0, 0)),
        pl.BlockSpec((1, PAGE_SIZE, wk), lambda i, pt: (i, 0, 0)),
        pl.BlockSpec((1, PAGE_SIZE, wv), lambda i, pt: (i, 0, 0)),
        pl.BlockSpec(memory_space=pl.ANY),
        pl.BlockSpec(memory_space=pl.ANY),
    ]
    mask_groups = 0
    if mask is not None:
        mask_groups = mask.shape[1]
        ins.append(mask)
        specs.append(pl.BlockSpec((1,) + mask.shape[1:], lambda i, pt: (i, 0, 0, 0, 0)))
    if bias is not None:
        ins.append(bias)
        specs.append(pl.BlockSpec(bias.shape, lambda i, pt: (0, 0, 0, 0)))
    if vproj is not None:
        ins.append(vproj)
        specs.append(pl.BlockSpec(vproj.shape, lambda i, pt: (0, 0, 0)))
    r = hg * tq
    grid_spec = pltpu.PrefetchScalarGridSpec(
        num_scalar_prefetch=1,
        grid=(b,),
        in_specs=specs,
        out_specs=pl.BlockSpec((1, tq, groups * hg * HEAD_DIM), lambda i, pt: (i, 0, 0)),
        scratch_shapes=[
            pltpu.VMEM((depth, tk, wk), F32),
            pltpu.VMEM((depth, tk, wv), F32),
            pltpu.SemaphoreType.DMA((depth, 2)),
            pltpu.VMEM((groups, r, dq), BF16),
            pltpu.VMEM((groups, r, V7X_LANES), F32),
            pltpu.VMEM((groups, r, dv + V7X_LANES), F32),
        ],
    )
    return pl.pallas_call(
        functools.partial(_paged_attn_kernel, kind=kind, slot=slot, nc=nc, nb=b, depth=depth, past=past, scale=scale,
                          mask_groups=mask_groups, has_bias=bias is not None, has_vproj=vproj is not None),
        out_shape=jax.ShapeDtypeStruct((b, tq, groups * hg * HEAD_DIM), F32),
        grid_spec=grid_spec,
        compiler_params=_cparams(("arbitrary",)),
        name=name,
    )(*ins)


def _rope_rows(x, cos, sin):
    half = x.shape[-1] // 2
    x1, x2 = x[:, :half], x[:, half:]
    return jnp.concatenate([x1 * cos - x2 * sin, x1 * sin + x2 * cos], axis=-1)


def _mla_prep_kernel(qf_ref, c_ref, kr_ref, gk_ref, wuk_ref, cos_ref, sin_ref, qcat_ref, ckr_ref, cout_ref, krout_ref):
    cos, sin = cos_ref[...], sin_ref[...]
    tq = cos.shape[0]
    c = c_ref[0]
    cn = c * lax.rsqrt(jnp.mean(c * c, axis=-1, keepdims=True) + RMS_EPS) * gk_ref[...]
    kr = _rope_rows(kr_ref[0][:, :B_ROPE], cos, sin)
    cout_ref[0] = cn
    krout_ref[0] = kr
    zpad = jnp.zeros((tq, MLA_QK - B_LORA - B_ROPE), BF16)
    ckr_ref[0] = jnp.concatenate([cn.astype(BF16), kr.astype(BF16), zpad], axis=1)
    hd = B_NOPE + B_ROPE
    for h in range(B_HEADS):
        qn = qf_ref[0, :, h * hd:h * hd + B_NOPE].astype(BF16)
        qr = _rope_rows(qf_ref[0, :, h * hd + B_NOPE:(h + 1) * hd], cos, sin)
        ql = jnp.dot(qn, wuk_ref[h], preferred_element_type=F32)
        qcat_ref[0, :, h * MLA_QK:(h + 1) * MLA_QK] = jnp.concatenate([ql.astype(BF16), qr.astype(BF16), zpad], axis=1)


def _mla_prep(z, kv_norm, wuk_t, cos, sin, *, tq):
    b, t = z.shape[0], z.shape[1]
    qw = B_HEADS * (B_NOPE + B_ROPE)
    half = B_ROPE // 2
    return pl.pallas_call(
        _mla_prep_kernel,
        out_shape=(
            jax.ShapeDtypeStruct((b, t, B_HEADS * MLA_QK), BF16),
            jax.ShapeDtypeStruct((b, t, MLA_QK), BF16),
            jax.ShapeDtypeStruct((b, t, B_LORA), F32),
            jax.ShapeDtypeStruct((b, t, B_ROPE), F32),
        ),
        grid=(b, t // tq),
        in_specs=[
            pl.BlockSpec((1, tq, qw), lambda bb, i: (bb, i, 0)),
            pl.BlockSpec((1, tq, B_LORA), lambda bb, i: (bb, i, qw // B_LORA)),
            pl.BlockSpec((1, tq, V7X_LANES), lambda bb, i: (bb, i, TAIL_COL // V7X_LANES)),
            pl.BlockSpec((1, B_LORA), lambda bb, i: (0, 0)),
            pl.BlockSpec(wuk_t.shape, lambda bb, i: (0, 0, 0)),
            pl.BlockSpec((tq, half), lambda bb, i: (i, 0)),
            pl.BlockSpec((tq, half), lambda bb, i: (i, 0)),
        ],
        out_specs=(
            pl.BlockSpec((1, tq, B_HEADS * MLA_QK), lambda bb, i: (bb, i, 0)),
            pl.BlockSpec((1, tq, MLA_QK), lambda bb, i: (bb, i, 0)),
            pl.BlockSpec((1, tq, B_LORA), lambda bb, i: (bb, i, 0)),
            pl.BlockSpec((1, tq, B_ROPE), lambda bb, i: (bb, i, 0)),
        ),
        compiler_params=_cparams(("parallel", "parallel")),
        name="mla_prep",
    )(z, z, z, kv_norm.reshape(1, B_LORA), wuk_t, cos, sin)


def _compress_lines(x, w1_ref, w2_ref, o_ref, *, n_sub, n_out):
    gw = C_GROUPS * HEAD_DIM
    first = jnp.zeros((n_sub, gw), F32)
    second = jnp.zeros((n_sub, gw), F32)
    for r in range(CMP_STRIDE):
        xr = x(r)
        first = first + xr * w1_ref[:, r * gw:(r + 1) * gw]
        second = second + xr * w2_ref[:, r * gw:(r + 1) * gw]
    blocks = (first + pltpu.roll(second, n_sub - 1, axis=0)) * (1.0 / CMP_LEN)
    o_ref[0, :n_sub, :] = blocks
    if n_out > n_sub:
        o_ref[0, n_sub:, :] = jnp.zeros((n_out - n_sub, gw), F32)


def _nsa_compress_kernel(xk_ref, xv_ref, wk1_ref, wk2_ref, wv1_ref, wv2_ref, ok_ref, ov_ref, *, n_sub, n_out):
    gw = C_GROUPS * HEAD_DIM
    for x_ref, w1_ref, w2_ref, o_ref in ((xk_ref, wk1_ref, wk2_ref, ok_ref), (xv_ref, wv1_ref, wv2_ref, ov_ref)):
        _compress_lines(lambda r, x_ref=x_ref: x_ref[0, :, r * gw:(r + 1) * gw], w1_ref, w2_ref, o_ref, n_sub=n_sub, n_out=n_out)


def _flat_cmp_weights(w):
    width = CMP_STRIDE * C_GROUPS * HEAD_DIM
    w2 = jnp.broadcast_to(w.reshape(2, CMP_STRIDE, 1, HEAD_DIM), (2, CMP_STRIDE, C_GROUPS, HEAD_DIM))
    w2 = w2.reshape(2, 1, width)
    return w2[0], w2[1]


def _nsa_compress_paged_kernel(pt_ref, nk_ref, nv_ref, pk_ref, pv_ref, wk1_ref, wk2_ref, wv1_ref, wv2_ref, ok_ref, ov_ref,
                               xk, xv, sem, *, slot, n_pages, n_sub, n_out, n_steps):
    s = pl.program_id(0)
    gw = C_GROUPS * HEAD_DIM
    lpp = PAGE_SIZE // CMP_STRIDE

    def each_page(step, si, fn):
        for a, (pool, buf) in enumerate(((pk_ref, xk), (pv_ref, xv))):
            cp = _page_copies(pt_ref, pool, buf.at[si], sem.at[si, a], slot=slot, row=step, n_pages=n_pages, rows_per_page=lpp)

            def body(p, carry):
                fn(cp(p))
                return carry
            lax.fori_loop(0, n_pages, body, 0)

    @pl.when(s == 0)
    def _():
        each_page(0, 0, lambda c: c.start())

    @pl.when(s + 1 < n_steps)
    def _():
        each_page(s + 1, (s + 1) % 2, lambda c: c.start())

    si = s % 2
    each_page(s, si, lambda c: c.wait())
    paged = n_pages * lpp
    for buf, n_ref, w1_ref, w2_ref, o_ref in ((xk, nk_ref, wk1_ref, wk2_ref, ok_ref), (xv, nv_ref, wv1_ref, wv2_ref, ov_ref)):
        buf[si, paged:, :] = n_ref[0]
        _compress_lines(lambda r, buf=buf: buf[si, :, r * gw:(r + 1) * gw], w1_ref, w2_ref, o_ref, n_sub=n_sub, n_out=n_out)


def _nsa_compress_paged(new_k, new_v, pool_k, pool_v, page_table, wk, wv, *, slot, n_out):
    b = new_k.shape[0]
    n_pages = page_table.shape[1]
    width = new_k.shape[2]
    lpp = PAGE_SIZE // CMP_STRIDE
    n_sub = n_pages * lpp + new_k.shape[1]
    wk1, wk2 = _flat_cmp_weights(wk)
    wv1, wv2 = _flat_cmp_weights(wv)
    wspec = pl.BlockSpec((1, width), lambda s, pt: (0, 0))
    nspec = pl.BlockSpec((1, new_k.shape[1], width), lambda s, pt: (s, 0, 0))
    ospec = pl.BlockSpec((1, n_out, C_GROUPS * HEAD_DIM), lambda s, pt: (s, 0, 0))
    grid_spec = pltpu.PrefetchScalarGridSpec(
        num_scalar_prefetch=1,
        grid=(b,),
        in_specs=[nspec, nspec, pl.BlockSpec(memory_space=pl.ANY), pl.BlockSpec(memory_space=pl.ANY)] + [wspec] * 4,
        out_specs=(ospec, ospec),
        scratch_shapes=[pltpu.VMEM((2, n_sub, width), F32), pltpu.VMEM((2, n_sub, width), F32), pltpu.SemaphoreType.DMA((2, 2))],
    )
    return pl.pallas_call(
        functools.partial(_nsa_compress_paged_kernel, slot=slot, n_pages=n_pages, n_sub=n_sub, n_out=n_out, n_steps=b),
        out_shape=(jax.ShapeDtypeStruct((b, n_out, C_GROUPS * HEAD_DIM), F32),) * 2,
        grid_spec=grid_spec,
        compiler_params=_cparams(("arbitrary",)),
        name="nsa_compress_dec",
    )(page_table, new_k, new_v, pool_k, pool_v, wk1, wk2, wv1, wv2)


def _nsa_compress(xk, xv, wk, wv, *, n_out):
    b, n_sub, width = xk.shape
    wk1, wk2 = _flat_cmp_weights(wk)
    wv1, wv2 = _flat_cmp_weights(wv)
    wspec = pl.BlockSpec((1, width), lambda bb: (0, 0))
    return pl.pallas_call(
        functools.partial(_nsa_compress_kernel, n_sub=n_sub, n_out=n_out),
        out_shape=(jax.ShapeDtypeStruct((b, n_out, C_GROUPS * HEAD_DIM), F32),) * 2,
        grid=(b,),
        in_specs=[pl.BlockSpec((1, n_sub, width), lambda bb: (bb, 0, 0))] * 2 + [wspec] * 4,
        out_specs=(pl.BlockSpec((1, n_out, C_GROUPS * HEAD_DIM), lambda bb: (bb, 0, 0)),) * 2,
        compiler_params=_cparams(("parallel",)),
        name="nsa_compress",
    )(xk, xv, wk1, wk2, wv1, wv2)


def _nsa_cmp_kernel(q_ref, kcb_ref, vcb_ref, bias_ref, m_ref, e_ref, o_ref, mask_ref, *, tq, tk, nb, ns_pad, n_slc, n_top, tkeys, qpos0):
    i = pl.program_id(1)
    t0 = qpos0 + i * tq
    hg = C_HEADS // C_GROUPS
    scale = HEAD_DIM ** -0.5
    r = hg * tq
    qrow = t0 + lax.broadcasted_iota(I32, (tq, nb), 0)
    cend = lax.broadcasted_iota(I32, (tq, nb), 1) * CMP_STRIDE + (CMP_LEN - 1)
    vis = jnp.where(cend <= qrow, 0.0, NEG)
    qs = t0 + lax.broadcasted_iota(I32, (tq, ns_pad), 0)
    blk = lax.broadcasted_iota(I32, (tq, ns_pad), 1)
    forced = (blk == qs // SLC_LEN) | (blk == 0)
    admissible = blk * SLC_LEN <= qs
    qk = t0 + lax.broadcasted_iota(I32, (tq, tkeys), 0)
    kpos = lax.broadcasted_iota(I32, (tq, tkeys), 1)
    for g in range(C_GROUPS):
        qg = jnp.concatenate([q_ref[0, :, (g * hg + h) * HEAD_DIM:(g * hg + h + 1) * HEAD_DIM] for h in range(hg)],
                             axis=0).astype(BF16)
        kg = kcb_ref[0, :, g * HEAD_DIM:(g + 1) * HEAD_DIM].astype(BF16)
        vg = vcb_ref[0, :, g * HEAD_DIM:(g + 1) * HEAD_DIM].astype(BF16)
        s = lax.dot_general(qg, kg, (((1,), (1,)), ((), ())), preferred_element_type=F32) * scale
        s = s + bias_ref[g, 0] + jnp.concatenate([vis] * hg, axis=0)
        mx = jnp.max(s, axis=1, keepdims=True)
        e = jnp.exp(s - mx)
        valid = mx > 0.5 * NEG
        p = e * jnp.where(valid, 1.0 / jnp.sum(e, axis=1, keepdims=True), 0.0)
        o = jnp.dot(p.astype(BF16), vg, preferred_element_type=F32)
        psum = p[0:tq]
        for h in range(hg):
            o_ref[0, :, (g * hg + h) * HEAD_DIM:(g * hg + h + 1) * HEAD_DIM] = o[h * tq:(h + 1) * tq]
            if h:
                psum = psum + p[h * tq:(h + 1) * tq]
        p_hi = psum.astype(BF16)
        p_lo = (psum - p_hi.astype(F32)).astype(BF16)
        imp = jnp.dot(p_hi, m_ref[...], preferred_element_type=F32) + jnp.dot(p_lo, m_ref[...], preferred_element_type=F32)
        imp = jnp.where(forced, FORCE_SCORE, jnp.where(admissible, imp, NEG))
        imp = jnp.where(blk < n_slc, imp, MINF)
        rank = jnp.zeros((tq, ns_pad), F32)
        for c in range(n_slc):
            col = imp[:, c:c + 1]
            rank = rank + jnp.where((col > imp) | ((col == imp) & (blk > c)), 1.0, 0.0)
        sel = jnp.where(rank < n_top, 1.0, 0.0).astype(BF16)
        hit = jnp.dot(sel, e_ref[...], preferred_element_type=F32)
        msk = jnp.where((hit > 0.5) & (kpos <= qk), 0.0, NEG)
        for c in range(tkeys // tk):
            mask_ref[0, g, c] = msk[:, c * tk:(c + 1) * tk]


def _nsa_cmp(q, kcb, vcb, bias_c, *, tq, tk, n_slc, tkeys, qpos0, q_col=0):
    b, tq_total = q.shape[0], q.shape[1]
    nb = kcb.shape[1]
    ns_pad = _round_up(n_slc, V7X_LANES)
    c0 = np.arange(nb)[:, None] * CMP_STRIDE
    s0 = np.arange(ns_pad)[None, :] * SLC_LEN
    shared = np.minimum(c0 + CMP_LEN, s0 + SLC_LEN) - np.maximum(c0, s0)
    m = np.maximum(shared, 0).astype(np.float32) / CMP_LEN
    m[:, n_slc:] = 0.0
    e = (np.arange(tkeys)[None, :] // SLC_LEN == np.arange(ns_pad)[:, None]).astype(np.float32)
    hq = C_HEADS * HEAD_DIM
    return pl.pallas_call(
        functools.partial(_nsa_cmp_kernel, tq=tq, tk=tk, nb=nb, ns_pad=ns_pad, n_slc=n_slc, n_top=min(SLC_TOPN, n_slc),
                          tkeys=tkeys, qpos0=qpos0),
        out_shape=(jax.ShapeDtypeStruct((b, tq_total, hq), F32), jax.ShapeDtypeStruct((b, C_GROUPS, tkeys // tk, tq_total, tk), F32)),
        grid=(b, tq_total // tq),
        in_specs=[
            pl.BlockSpec((1, tq, hq), lambda bb, i: (bb, i, q_col)),
            pl.BlockSpec((1, nb, C_GROUPS * HEAD_DIM), lambda bb, i: (bb, 0, 0)),
            pl.BlockSpec((1, nb, C_GROUPS * HEAD_DIM), lambda bb, i: (bb, 0, 0)),
            pl.BlockSpec((C_GROUPS, 1, (C_HEADS // C_GROUPS) * tq, nb), lambda bb, i: (0, i, 0, 0)),
            pl.BlockSpec((nb, ns_pad), lambda bb, i: (0, 0)),
            pl.BlockSpec((ns_pad, tkeys), lambda bb, i: (0, 0)),
        ],
        out_specs=(
            pl.BlockSpec((1, tq, hq), lambda bb, i: (bb, i, 0)),
            pl.BlockSpec((1, C_GROUPS, tkeys // tk, tq, tk), lambda bb, i: (bb, 0, 0, i, 0)),
        ),
        compiler_params=_cparams(("parallel", "parallel")),
        name="nsa_cmp",
    )(q, kcb, vcb, bias_c, jnp.asarray(m, BF16), jnp.asarray(e, BF16))


def _nsa_combine_kernel(oc_ref, os_ref, ow_ref, g_ref, gb_ref, o_ref):
    g = g_ref[...][:, :3 * C_HEADS] + gb_ref[...]
    gate = 1.0 / (1.0 + jnp.exp(-g))
    for h in range(C_HEADS):
        sl = slice(h * HEAD_DIM, (h + 1) * HEAD_DIM)
        o_ref[:, sl] = (oc_ref[:, sl] * gate[:, 3 * h:3 * h + 1] + os_ref[:, sl] * gate[:, 3 * h + 1:3 * h + 2]
                        + ow_ref[:, sl] * gate[:, 3 * h + 2:3 * h + 3])


def _nsa_combine(o_cmp, o_slc, o_win, z, gate_b):
    m, d = o_cmp.shape
    tm = min(m, 512)
    ospec = pl.BlockSpec((tm, d), lambda i: (i, 0))
    return pl.pallas_call(
        _nsa_combine_kernel,
        out_shape=jax.ShapeDtypeStruct((m, d), F32),
        grid=(m // tm,),
        in_specs=[ospec, ospec, ospec, pl.BlockSpec((tm, V7X_LANES), lambda i: (i, TAIL_COL // V7X_LANES)),
                  pl.BlockSpec((1, 3 * C_HEADS), lambda i: (0, 0))],
        out_specs=ospec,
        compiler_params=_cparams(("parallel",)),
        name="nsa_combine",
    )(o_cmp, o_slc, o_win, z, gate_b.reshape(1, 3 * C_HEADS))


def _t5_bucket(dist):
    n = jnp.maximum(dist, 0)
    exact = N_BUCKETS // 2
    nf = jnp.maximum(n, 1).astype(F32)
    large = exact + (jnp.log(nf / exact) / math.log(MAX_DISTANCE / exact) * (N_BUCKETS - exact)).astype(I32)
    return jnp.where(n < exact, n, jnp.minimum(large, N_BUCKETS - 1))


def _bias_by_distance(rel_bias):
    return rel_bias[_t5_bucket(jnp.arange(2 * V7X_LANES))]


def _bias_tiles(f, tq, groups):
    a = jnp.arange(tq)[:, None]
    c = jnp.arange(V7X_LANES)[None, :]
    t0 = f[jnp.clip(a - c, 0, 2 * V7X_LANES - 1)]
    t1 = f[V7X_LANES + a - c]
    t2 = jnp.broadcast_to(f[2 * V7X_LANES - 1], t0.shape)
    tiles = jnp.stack([t0, t1, t2], axis=0)
    heads = f.shape[1]
    tiles = tiles.transpose(3, 0, 1, 2).reshape(groups, heads // groups, 3, tq, V7X_LANES)
    return tiles.transpose(0, 2, 1, 3, 4).reshape(groups, 3, (heads // groups) * tq, V7X_LANES)


def _bias_cmp(f, qpos, nb, tq, groups):
    dist = qpos[:, None] - (jnp.arange(nb)[None, :] * CMP_STRIDE + CMP_LEN - 1)
    bc = f[jnp.clip(dist, 0, 2 * V7X_LANES - 1)]
    heads = f.shape[1]
    nq = qpos.shape[0] // tq
    bc = bc.reshape(nq, tq, nb, groups, heads // groups).transpose(3, 0, 4, 1, 2)
    return bc.reshape(groups, nq, (heads // groups) * tq, nb)


def _rope_tables(pos):
    half = B_ROPE // 2
    inv = ROPE_THETA ** (-jnp.arange(half, dtype=F32) / half)
    ang = pos.astype(F32)[:, None] * inv[None, :]
    return jnp.cos(ang), jnp.sin(ang)


def _pad_cols(w):
    return jnp.pad(w, ((0, 0), (0, W_IN_PAD - w.shape[1])))


def _pad_rows(x, rows):
    return jnp.pad(x, ((0, 0), (0, rows - x.shape[1])) + ((0, 0),) * (x.ndim - 2))


TK = 512
SELECT_BATCH_DEC = 4


def _merge_groups(pool):
    return pool.reshape(pool.shape[:3] + (-1,))


def _mixer_a(z, f_bias, *, decode, caches=None, slot=0, page_table=None):
    b, tq_total = z.shape[0], z.shape[1]
    kcol = A_HEADS * HEAD_DIM
    k_new = z[:, :, kcol:kcol + A_KV_HEADS * HEAD_DIM]
    v_new = z[:, :, kcol + A_KV_HEADS * HEAD_DIM:kcol + 2 * A_KV_HEADS * HEAD_DIM]
    ki_new = z[:, :, TAIL_COL:TAIL_COL + IDX_DIM]
    hg = A_HEADS // A_KV_HEADS
    if not decode:
        tq = V7X_LANES
        t = tq_total
        n_chunks = t // TK
        mask = _dsa_select(z, tq=tq, tk=TK, n_keep=min(IDX_TOPK, t // 4))
        bias = _bias_tiles(f_bias, tq, A_KV_HEADS)
        o = _fa(z, z, z, groups=A_KV_HEADS, hg=hg, dq=HEAD_DIM, dv=HEAD_DIM, tq=tq, tk=TK, nkc=n_chunks,
                n_chunks=n_chunks, scale=HEAD_DIM ** -0.5, mode="mask",
                k_col=lambda n: kcol // HEAD_DIM + n, v_col=lambda n: kcol // HEAD_DIM + A_KV_HEADS + n,
                mask=mask, bias=bias, name="dsa_attn")
    else:
        cache_k, cache_v, cache_idx = caches
        past = page_table.shape[1] * PAGE_SIZE
        tq = ROW_TILE_DEC
        z8 = _pad_rows(z, tq)
        mask = _dsa_select_paged(z8, _pad_rows(ki_new, PAGE_SIZE), cache_idx, page_table, slot=slot, tb=SELECT_BATCH_DEC, tk=TK,
                                 n_keep=min(IDX_TOPK, (past + tq_total) // 4))
        bias = _bias_tiles(f_bias, tq, A_KV_HEADS)
        o = _paged_attn(z8, _pad_rows(k_new, PAGE_SIZE), _pad_rows(v_new, PAGE_SIZE), _merge_groups(cache_k), _merge_groups(cache_v),
                        page_table, kind="gqa", slot=slot, scale=HEAD_DIM ** -0.5, tk=TK, mask=mask, bias=bias, name="dsa_attn_dec")
        o = o[:, :tq_total]
    return o.reshape(b * tq_total, A_HEADS * HEAD_DIM), (k_new, v_new, ki_new)


def _mixer_b(z, kv_norm, w_kvb, *, decode, caches=None, slot=0, page_table=None):
    b, tq_total = z.shape[0], z.shape[1]
    w = w_kvb.reshape(B_LORA, B_HEADS, B_NOPE + B_V)
    wuk_t = w[..., :B_NOPE].transpose(1, 2, 0).astype(BF16)
    wuv = w[..., B_NOPE:].transpose(1, 0, 2).astype(BF16)
    scale = (B_NOPE + B_ROPE) ** -0.5
    if not decode:
        tq = V7X_LANES
        cos, sin = _rope_tables(jnp.arange(tq_total))
        qcat, ckr, c_new, kr_new = _mla_prep(z, kv_norm, wuk_t, cos, sin, tq=tq)
        n_chunks = tq_total // TK
        o = _fa(qcat, ckr, ckr, groups=1, hg=B_HEADS, dq=MLA_QK, dv=B_LORA, tq=tq, tk=TK, nkc=n_chunks, n_chunks=n_chunks,
                scale=scale, mode="causal", q_col=lambda n: 0, k_col=lambda n: 0, v_col=lambda n: 0, vproj=wuv, name="mla_attn")
    else:
        cache_lat, cache_rope = caches
        past = page_table.shape[1] * PAGE_SIZE
        tq = ROW_TILE_DEC
        z8 = _pad_rows(z, tq)
        cos, sin = _rope_tables(past + jnp.arange(tq))
        qcat, ckr8, c8, kr8 = _mla_prep(z8, kv_norm, wuk_t, cos, sin, tq=tq)
        c_new, kr_new = c8[:, :tq_total], kr8[:, :tq_total]
        o = _paged_attn(qcat, _pad_rows(c_new, PAGE_SIZE), _pad_rows(kr_new, PAGE_SIZE), cache_lat, cache_rope, page_table,
                        kind="mla", slot=slot, scale=scale, tk=TK, vproj=wuv, name="mla_attn_dec")
        o = o[:, :tq_total]
    return o.reshape(b * tq_total, B_HEADS * B_V), (c_new, kr_new)


def _mixer_c(z, gate_b, wk_pos, wv_pos, f_bias, *, decode, caches=None, slot=0, win=None, page_table=None):
    b, tq_total = z.shape[0], z.shape[1]
    gw = C_GROUPS * HEAD_DIM
    q0 = C_HEADS * HEAD_DIM
    kc, vc, ks, vs, kw, vw = (z[:, :, q0 + j * gw:q0 + (j + 1) * gw] for j in range(6))
    hg = C_HEADS // C_GROUPS
    scale = HEAD_DIM ** -0.5
    line = CMP_STRIDE * gw
    if not decode:
        t = tq_total
        tq = V7X_LANES
        n_sub = t // CMP_STRIDE
        kcb, vcb = _nsa_compress(kc.reshape(b, n_sub, line), vc.reshape(b, n_sub, line), wk_pos, wv_pos, n_out=n_sub)
        n_slc = -(-t // SLC_LEN)
        bias_c = _bias_cmp(f_bias, jnp.arange(t), n_sub, tq, C_GROUPS)
        o_cmp, mask = _nsa_cmp(z, kcb, vcb, bias_c, tq=tq, tk=TK, n_slc=n_slc, tkeys=t, qpos0=0)
        bias = _bias_tiles(f_bias, tq, C_GROUPS)
        n_chunks = t // TK
        blk = lambda j: (lambda n: (q0 + j * gw) // HEAD_DIM + n)
        o_slc = _fa(z, z, z, groups=C_GROUPS, hg=hg, dq=HEAD_DIM, dv=HEAD_DIM, tq=tq, tk=TK, nkc=n_chunks, n_chunks=n_chunks,
                    scale=scale, mode="mask", k_col=blk(2), v_col=blk(3), mask=mask, bias=bias, name="nsa_slc")
        tkw = V7X_LANES
        o_win = _fa(z, z, z, groups=C_GROUPS, hg=hg, dq=HEAD_DIM, dv=HEAD_DIM, tq=tq, tk=tkw, nkc=WINDOW // tkw + 1,
                    n_chunks=t // tkw, scale=scale, mode="window", k_col=blk(4), v_col=blk(5), bias=bias, name="nsa_win")
        n_w = min(WINDOW, t)
        win_new = (kw[:, t - n_w:], vw[:, t - n_w:])
        zrows = z.reshape(b * t, W_IN_PAD)
    else:
        cache_kc, cache_vc, cache_ks, cache_vs = caches
        win_k, win_v = win
        past = page_table.shape[1] * PAGE_SIZE
        tq = ROW_TILE_DEC
        z8 = _pad_rows(z, tq)
        lpp = PAGE_SIZE // CMP_STRIDE
        n_sub = past // CMP_STRIDE + V7X_SUBLANES
        new_lines = lambda x: _pad_rows(_pad_rows(x, CMP_STRIDE).reshape(b, 1, line), V7X_SUBLANES)
        lines = lambda pool: pool.reshape(pool.shape[:2] + (lpp, line))
        nb = _round_up(n_sub, V7X_LANES)
        kcb, vcb = _nsa_compress_paged(new_lines(kc), new_lines(vc), lines(cache_kc), lines(cache_vc), page_table, wk_pos, wv_pos,
                                       slot=slot, n_out=nb)
        n_slc = -(-(past + tq_total) // SLC_LEN)
        tkeys = past + TK
        bias_c = _bias_cmp(f_bias, past + jnp.arange(tq), nb, tq, C_GROUPS)
        o_cmp, mask = _nsa_cmp(z8, kcb, vcb, bias_c, tq=tq, tk=TK, n_slc=n_slc, tkeys=tkeys, qpos0=past)
        bias = _bias_tiles(f_bias, tq, C_GROUPS)
        o_slc = _paged_attn(z8, _pad_rows(ks, PAGE_SIZE), _pad_rows(vs, PAGE_SIZE), _merge_groups(cache_ks), _merge_groups(cache_vs),
                            page_table, kind="gqa", slot=slot, scale=scale, tk=TK, mask=mask, bias=bias, name="nsa_slc_dec")
        n_buf = win_k.shape[1]
        tkw = V7X_LANES
        wrows = _round_up(n_buf + tq, tkw)
        kw_cat = jnp.concatenate([win_k.reshape(b, n_buf, gw), kw], axis=1)
        vw_cat = jnp.concatenate([win_v.reshape(b, n_buf, gw), vw], axis=1)
        o_win = _fa(z8, _pad_rows(kw_cat, wrows), _pad_rows(vw_cat, wrows), groups=C_GROUPS, hg=hg, dq=HEAD_DIM, dv=HEAD_DIM,
                    tq=tq, tk=tkw, nkc=wrows // tkw, n_chunks=wrows // tkw, scale=scale, mode="window", qpos0=past,
                    kpos0=past - n_buf, bias=bias, name="nsa_win_dec")
        n_w = min(WINDOW, n_buf + tq_total)
        win_new = (kw_cat[:, -n_w:], vw_cat[:, -n_w:])
        o_cmp, o_slc, o_win = (x[:, :tq_total] for x in (o_cmp, o_slc, o_win))
        zrows = z.reshape(b * tq_total, W_IN_PAD)
    hq = C_HEADS * HEAD_DIM
    o = _nsa_combine(o_cmp.reshape(-1, hq), o_slc.reshape(-1, hq), o_win.reshape(-1, hq), zrows, gate_b)
    return o, (kc, vc, ks, vs) + win_new


def kernel(x_prompt, x_sample, cache_a_k, cache_a_v, cache_a_idx, cache_b_latent, cache_b_rope, cache_c_cmp_k, cache_c_cmp_v, cache_c_slc_k, cache_c_slc_v, state_c_win_k, state_c_win_v, state_ffn_conv, page_table, rel_bias, attn_norm, ffn_norm, final_norm, a_w_in, a_w_o, b_w_in, b_kv_norm, b_w_kvb, b_w_o, c_w_in, c_gate_b, c_cmp_wk, c_cmp_wv, c_w_o, ffn_w_up, ffn_conv_w, ffn_conv_b, ffn_w_down):
    bp, t, d = x_prompt.shape
    bs, ts, _ = x_sample.shape
    hp = x_prompt.reshape(bp * t, d)
    hs = x_sample.reshape(bs * ts, d)
    f_bias = _bias_by_distance(rel_bias)
    st_p = {"a": [], "b": [], "c": [], "conv": []}
    st_s = {"a": [], "b": [], "c": [], "conv": []}
    for i in range(DEPTH):
        kind, slot = i % 3, i // 3
        w_in = _pad_cols((a_w_in, b_w_in, c_w_in)[kind][slot])
        w_o = (a_w_o, b_w_o, c_w_o)[kind][slot]
        zp = _mm(hp, w_in, norm_g=attn_norm[i], tn=768, name="in_proj").reshape(bp, t, W_IN_PAD)
        zs = _mm(hs, w_in, norm_g=attn_norm[i], tn=768, name="in_proj_dec").reshape(bs, ts, W_IN_PAD)
        if kind == 0:
            op, sp = _mixer_a(zp, f_bias, decode=False)
            os_, ss = _mixer_a(zs, f_bias, decode=True, caches=(cache_a_k, cache_a_v, cache_a_idx), slot=slot, page_table=page_table)
            key = "a"
        elif kind == 1:
            op, sp = _mixer_b(zp, b_kv_norm[slot], b_w_kvb[slot], decode=False)
            os_, ss = _mixer_b(zs, b_kv_norm[slot], b_w_kvb[slot], decode=True, caches=(cache_b_latent, cache_b_rope), slot=slot,
                               page_table=page_table)
            key = "b"
        else:
            op, sp = _mixer_c(zp, c_gate_b[slot], c_cmp_wk[slot], c_cmp_wv[slot], f_bias, decode=False)
            os_, ss = _mixer_c(zs, c_gate_b[slot], c_cmp_wk[slot], c_cmp_wv[slot], f_bias, decode=True,
                               caches=(cache_c_cmp_k, cache_c_cmp_v, cache_c_slc_k, cache_c_slc_v), slot=slot,
                               win=(state_c_win_k[slot], state_c_win_v[slot]), page_table=page_table)
            key = "c"
        st_p[key].append(sp)
        st_s[key].append(ss)
        hp = _mm(op, w_o, res=hp, name="out_proj")
        hs = _mm(os_, w_o, res=hs, name="out_proj_dec")
        fp, tail = _ffn_up(hp, ffn_norm[i], ffn_w_up[i], ffn_conv_w[i], ffn_conv_b[i], seq_len=t)
        fs, gs = _ffn_up(hs, ffn_norm[i], ffn_w_up[i], ffn_conv_w[i], ffn_conv_b[i], seq_len=ts, prev=state_ffn_conv[i])
        hp = _mm(fp, ffn_w_down[i], res=hp, name="ffn_down")
        hs = _mm(fs, ffn_w_down[i], res=hs, name="ffn_down_dec")
        blocks_per_seq = tail.shape[0] // bp
        st_p["conv"].append(tail[blocks_per_seq - 1::blocks_per_seq, V7X_SUBLANES - (CONV_W - 1):, :])
        st_s["conv"].append(gs.reshape(bs, ts, D_FF)[:, ts - (CONV_W - 1):])
    y_prompt = _rmsnorm(hp, final_norm).reshape(bp, t, d)
    y_sample = _rmsnorm(hs, final_norm).reshape(bs, ts, d)

    def stack(states, j, shape_tail):
        return jnp.stack([s[j].reshape(s[j].shape[:2] + shape_tail) for s in states])

    def group(st):
        kv = (A_KV_HEADS, HEAD_DIM)
        cg = (C_GROUPS, HEAD_DIM)
        return (stack(st["a"], 0, kv), stack(st["a"], 1, kv), stack(st["a"], 2, (IDX_DIM,)),
                stack(st["b"], 0, (B_LORA,)), stack(st["b"], 1, (B_ROPE,)),
                stack(st["c"], 0, cg), stack(st["c"], 1, cg), stack(st["c"], 2, cg), stack(st["c"], 3, cg),
                stack(st["c"], 4, cg), stack(st["c"], 5, cg), jnp.stack(st["conv"]))

    return (y_prompt, y_sample) + group(st_p) + group(st_s)
```

```python
import functools
import math

import jax
import jax.numpy as jnp
import numpy as np
from jax import lax
from jax.experimental import pallas as pl
from jax.experimental.pallas import tpu as pltpu

F32 = jnp.float32
BF16 = jnp.bfloat16
I32 = jnp.int32

D_MODEL = 2048
DEPTH = 4
PAGE_SIZE = 128
HEAD_DIM = 128
RMS_EPS = 1e-6
NEG = -1e30
FORCE_SCORE = 1e9
N_BUCKETS = 32
MAX_DISTANCE = 128
A_HEADS = 16
A_KV_HEADS = 2
IDX_HEADS = 16
IDX_DIM = 64
IDX_TOPK = 256
B_HEADS = 16
B_NOPE = 128
B_ROPE = 64
B_V = 128
B_LORA = 512
ROPE_THETA = 10000.0
C_HEADS = 16
C_GROUPS = 2
CMP_LEN = 32
CMP_STRIDE = 16
SLC_LEN = 64
SLC_TOPN = 16
WINDOW = 512
D_FF = 4096
CONV_W = 3

V7X_LANES = 128
V7X_SUBLANES = 8
V7X_VMEM_BYTES = 64 * 1024 * 1024
VMEM_LIMIT_BYTES = V7X_VMEM_BYTES - 8 * 1024 * 1024

W_IN_PAD = 3840
TAIL_COL = 3584
ROW_TILE_DEC = 8
MLA_QK = 640
BIAS_SAT = 113
MINF = -3.0e38


def _cparams(sem):
    return pltpu.CompilerParams(dimension_semantics=sem, vmem_limit_bytes=VMEM_LIMIT_BYTES)


def _round_up(x, m):
    return (x + m - 1) // m * m


def _mm_kernel(*refs, has_norm, has_res, cast_a):
    it = iter(refs)
    a_ref = next(it)
    g_ref = next(it) if has_norm else None
    w_ref = next(it)
    r_ref = next(it) if has_res else None
    o_ref = next(it)
    abf_ref = next(it) if cast_a else None
    if cast_a:
        @pl.when(pl.program_id(1) == 0)
        def _():
            a = a_ref[...].astype(F32)
            if has_norm:
                a = a * lax.rsqrt(jnp.mean(a * a, axis=-1, keepdims=True) + RMS_EPS) * g_ref[...]
            abf_ref[...] = a.astype(BF16)
        a = abf_ref[...]
    else:
        a = a_ref[...]
    acc = jnp.dot(a, w_ref[...].astype(BF16), preferred_element_type=F32)
    if has_res:
        acc = acc + r_ref[...]
    o_ref[...] = acc.astype(o_ref.dtype)


def _mm(a, w, *, layer=None, norm_g=None, res=None, tn=512, name="mm"):
    m, k = a.shape
    n = w.shape[-1]
    tm = min(m, 1024)
    assert m % tm == 0 and n % tn == 0
    has_norm, has_res = norm_g is not None, res is not None
    cast_a = has_norm or a.dtype != BF16
    ins = [a]
    specs = [pl.BlockSpec((tm, k), lambda i, j: (i, 0))]
    if has_norm:
        ins.append(norm_g.reshape(1, k))
        specs.append(pl.BlockSpec((1, k), lambda i, j: (0, 0)))
    ins.append(w)
    if layer is None:
        specs.append(pl.BlockSpec((k, tn), lambda i, j: (0, j)))
    else:
        specs.append(pl.BlockSpec((None, k, tn), lambda i, j: (layer, 0, j)))
    if has_res:
        ins.append(res)
        specs.append(pl.BlockSpec((tm, tn), lambda i, j: (i, j)))
    return pl.pallas_call(
        functools.partial(_mm_kernel, has_norm=has_norm, has_res=has_res, cast_a=cast_a),
        out_shape=jax.ShapeDtypeStruct((m, n), F32),
        grid=(m // tm, n // tn),
        in_specs=specs,
        out_specs=pl.BlockSpec((tm, tn), lambda i, j: (i, j)),
        scratch_shapes=[pltpu.VMEM((tm, k), BF16)] if cast_a else [],
        compiler_params=_cparams(("parallel", "arbitrary")),
        name=name,
    )(*ins)


def _rmsnorm_kernel(x_ref, g_ref, o_ref):
    x = x_ref[...]
    o_ref[...] = x * lax.rsqrt(jnp.mean(x * x, axis=-1, keepdims=True) + RMS_EPS) * g_ref[...]


def _rmsnorm(x, g):
    m, d = x.shape
    tm = min(m, 1024)
    return pl.pallas_call(
        _rmsnorm_kernel,
        out_shape=jax.ShapeDtypeStruct((m, d), F32),
        grid=(m // tm,),
        in_specs=[pl.BlockSpec((tm, d), lambda i: (i, 0)), pl.BlockSpec((1, d), lambda i: (0, 0))],
        out_specs=pl.BlockSpec((tm, d), lambda i: (i, 0)),
        compiler_params=_cparams(("parallel",)),
        name="final_norm",
    )(x, g.reshape(1, d))


def _ffn_up_kernel(*refs, tm, blocks_per_seq, decode):
    if decode:
        x_ref, gn_ref, wg_ref, wu_ref, cw_ref, cb_ref, p1_ref, p2_ref, h_ref, g_ref, xn_ref = refs
    else:
        x_ref, gn_ref, wg_ref, wu_ref, cw_ref, cb_ref, h_ref, tail_ref, xn_ref, carry_ref = refs
    i = pl.program_id(0)
    j = pl.program_id(1)

    @pl.when(j == 0)
    def _():
        x = x_ref[...]
        xn_ref[...] = (x * lax.rsqrt(jnp.mean(x * x, axis=-1, keepdims=True) + RMS_EPS) * gn_ref[...]).astype(BF16)

    xn = xn_ref[...]
    g = jnp.dot(xn, wg_ref[...].astype(BF16), preferred_element_type=F32)
    u = jnp.dot(xn, wu_ref[...].astype(BF16), preferred_element_type=F32)
    row = lax.broadcasted_iota(I32, g.shape, 0)
    r1 = pltpu.roll(g, 1, axis=0)
    r2 = pltpu.roll(g, 2, axis=0)
    if decode:
        t = row & 3
        gm1 = jnp.where(t >= 1, r1, p1_ref[...])
        gm2 = jnp.where(t >= 2, r2, p2_ref[...])
        g_ref[...] = g
    else:
        tf = g.shape[1]
        @pl.when(i == 0)
        def _():
            carry_ref[j] = jnp.zeros((V7X_SUBLANES, tf), F32)

        c = jnp.where(i % blocks_per_seq == 0, 0.0, carry_ref[j])
        c1 = c[V7X_SUBLANES - 1:V7X_SUBLANES, :]
        c2 = c[V7X_SUBLANES - 2:V7X_SUBLANES - 1, :]
        gm1 = jnp.where(row == 0, c1, r1)
        gm2 = jnp.where(row == 0, c2, jnp.where(row == 1, c1, r2))
        tail = g[tm - V7X_SUBLANES:, :]
        carry_ref[j] = tail
        tail_ref[0] = tail
    cw = cw_ref[...]
    conv = cb_ref[...] + gm2 * cw[0:1, :] + gm1 * cw[1:2, :] + g * cw[2:3, :]
    h = conv * (1.0 / (1.0 + jnp.exp(-conv))) * u
    h_ref[...] = h.astype(BF16)


def _ffn_up(x, gn, w_up, layer, conv_w, conv_b, *, seq_len, prev=None, tf=512):
    m, d = x.shape
    nf = D_FF // tf
    decode = prev is not None
    tm = min(m, 1024)
    ins = [x, gn.reshape(1, d), w_up, w_up, conv_w, conv_b.reshape(1, D_FF)]
    specs = [
        pl.BlockSpec((tm, d), lambda i, j: (i, 0)),
        pl.BlockSpec((1, d), lambda i, j: (0, 0)),
        pl.BlockSpec((None, d, tf), lambda i, j: (layer, 0, j)),
        pl.BlockSpec((None, d, tf), lambda i, j: (layer, 0, j + nf)),
        pl.BlockSpec((CONV_W, tf), lambda i, j: (0, j)),
        pl.BlockSpec((1, tf), lambda i, j: (0, j)),
    ]
    if decode:
        assert seq_len == 4 and m == tm
        zero = jnp.zeros_like(prev[:, :1])
        p1 = jnp.concatenate([prev[:, 1:2], zero, zero, zero], axis=1).reshape(m, D_FF)
        p2 = jnp.concatenate([prev[:, 0:1], prev[:, 1:2], zero, zero], axis=1).reshape(m, D_FF)
        ins += [p1, p2]
        specs += [pl.BlockSpec((tm, tf), lambda i, j: (i, j))] * 2
        out_shape = (jax.ShapeDtypeStruct((m, D_FF), BF16), jax.ShapeDtypeStruct((m, D_FF), F32))
        out_specs = (pl.BlockSpec((tm, tf), lambda i, j: (i, j)), pl.BlockSpec((tm, tf), lambda i, j: (i, j)))
        scratch = [pltpu.VMEM((tm, d), BF16)]
        bps = 1
    else:
        assert seq_len % tm == 0
        bps = seq_len // tm
        out_shape = (jax.ShapeDtypeStruct((m, D_FF), BF16), jax.ShapeDtypeStruct((m // tm, V7X_SUBLANES, D_FF), F32))
        out_specs = (pl.BlockSpec((tm, tf), lambda i, j: (i, j)), pl.BlockSpec((1, V7X_SUBLANES, tf), lambda i, j: (i, 0, j)))
        scratch = [pltpu.VMEM((tm, d), BF16), pltpu.VMEM((nf, V7X_SUBLANES, tf), F32)]
    return pl.pallas_call(
        functools.partial(_ffn_up_kernel, tm=tm, blocks_per_seq=bps, decode=decode),
        out_shape=out_shape,
        grid=(m // tm, nf),
        in_specs=specs,
        out_specs=out_specs,
        scratch_shapes=scratch,
        compiler_params=_cparams(("arbitrary", "arbitrary")),
        name="ffn_up_dec" if decode else "ffn_up",
    )(*ins)


def _fa_chunk(i, kc, *, mode, tq, tk, qpos0, kpos0, n_chunks):
    if mode == "window":
        first = (qpos0 + i * tq - WINDOW + 1 - kpos0) // tk
        return first + kc
    return kc


def _fa_chunk_clamped(i, kc, **kw):
    c = _fa_chunk(i, kc, **kw)
    if kw["mode"] != "window":
        last = (kw["qpos0"] + i * kw["tq"] + kw["tq"] - 1 - kw["kpos0"]) // kw["tk"]
        c = jnp.minimum(c, last)
    return jnp.clip(c, 0, kw["n_chunks"] - 1)


def _fa_kernel(*refs, hg, dq, dv, tq, tk, nkc, scale, mode, qpos0, kpos0, n_chunks,
               has_mask, has_bias, has_vproj):
    it = iter(refs)
    q_ref, k_ref, v_ref = next(it), next(it), next(it)
    mask_ref = next(it) if has_mask else None
    bias_ref = next(it) if has_bias else None
    wv_ref = next(it) if has_vproj else None
    o_ref, qs_ref, m_ref, acc_ref = next(it), next(it), next(it), next(it)
    i = pl.program_id(1)
    n = pl.program_id(2)
    kc = pl.program_id(3)
    t0 = qpos0 + i * tq
    geo = dict(mode=mode, tq=tq, tk=tk, qpos0=qpos0, kpos0=kpos0, n_chunks=n_chunks)

    @pl.when(kc == 0)
    def _():
        for h in range(hg):
            qs_ref[h * tq:(h + 1) * tq, :] = q_ref[0, :, h * dq:(h + 1) * dq].astype(BF16)
        m_ref[...] = jnp.full(m_ref.shape, NEG, F32)
        acc_ref[...] = jnp.zeros(acc_ref.shape, F32)

    craw = _fa_chunk(i, kc, **geo)
    s0 = kpos0 + craw * tk
    needed = (craw >= 0) & (craw < n_chunks) & (s0 <= t0 + tq - 1)

    @pl.when(needed)
    def _():
        kb = k_ref[0].astype(BF16)
        s = lax.dot_general(qs_ref[...], kb, (((1,), (1,)), ((), ())), preferred_element_type=F32) * scale
        if has_bias:
            parts = []
            for u in range(tk // V7X_LANES):
                dsel = jnp.clip(t0 // V7X_LANES - (s0 // V7X_LANES + u), 0, 2)
                parts.append(bias_ref[n, dsel])
            s = s + (parts[0] if len(parts) == 1 else jnp.concatenate(parts, axis=1))
        add = None
        if has_mask:
            add = mask_ref[0, 0, 0]
        if mode in ("causal", "window"):
            qp = t0 + lax.broadcasted_iota(I32, (tq, tk), 0)
            kp = s0 + lax.broadcasted_iota(I32, (tq, tk), 1)
            ok = kp <= qp
            if mode == "window":
                ok = ok & (kp > qp - WINDOW) & (kp >= 0)
            add = jnp.where(ok, 0.0, NEG)
        if add is not None:
            s = s + (add if hg == 1 else jnp.concatenate([add] * hg, axis=0))
        m_old = m_ref[...]
        m_new = jnp.maximum(m_old, jnp.max(s, axis=1, keepdims=True))
        alpha = jnp.exp(m_old - m_new)
        p = jnp.exp(s - jnp.concatenate([m_new] * (tk // V7X_LANES), axis=1))
        vb = v_ref[0].astype(BF16)
        vext = jnp.concatenate([vb, jnp.ones((tk, V7X_LANES), BF16)], axis=1)
        pv = jnp.dot(p.astype(BF16), vext, preferred_element_type=F32)
        acc_ref[...] = acc_ref[...] * jnp.concatenate([alpha] * (dv // V7X_LANES + 1), axis=1) + pv
        m_ref[...] = m_new

    @pl.when(kc == nkc - 1)
    def _():
        valid = m_ref[...] > 0.5 * NEG
        l = jnp.where(valid, acc_ref[:, dv:dv + V7X_LANES], 1.0)
        inv = jnp.where(valid, 1.0 / l, 0.0)
        o = acc_ref[:, :dv] * jnp.concatenate([inv] * (dv // V7X_LANES), axis=1)
        for h in range(hg):
            oh = o[h * tq:(h + 1) * tq, :]
            if has_vproj:
                oh = jnp.dot(oh.astype(BF16), wv_ref[h], preferred_element_type=F32)
            o_ref[0, :, h * HEAD_DIM:(h + 1) * HEAD_DIM] = oh


def _fa(q, k, v, *, groups, hg, dq, dv, tq, tk, nkc, n_chunks, scale, mode, qpos0=0, kpos0=0,
        q_col=lambda n: n, k_col=lambda n: n, v_col=lambda n: n, mask=None, bias=None, vproj=None, name="fa"):
    b, tq_total = q.shape[0], q.shape[1]
    nq = tq_total // tq
    geo = dict(mode=mode, tq=tq, tk=tk, qpos0=qpos0, kpos0=kpos0, n_chunks=n_chunks)
    kmap = lambda col: (lambda bb, i, n, kc: (bb, _fa_chunk_clamped(i, kc, **geo), col(n)))
    ins = [q, k, v]
    specs = [
        pl.BlockSpec((1, tq, hg * dq), lambda bb, i, n, kc: (bb, i, q_col(n))),
        pl.BlockSpec((1, tk, dq), kmap(k_col)),
        pl.BlockSpec((1, tk, dv), kmap(v_col)),
    ]
    if mask is not None:
        ins.append(mask)
        gsel = (lambda n: n) if mask.shape[1] > 1 else (lambda n: 0)
        specs.append(pl.BlockSpec((1, 1, 1, tq, tk), lambda bb, i, n, kc: (bb, gsel(n), _fa_chunk_clamped(i, kc, **geo), i, 0)))
    if bias is not None:
        ins.append(bias)
        specs.append(pl.BlockSpec(bias.shape, lambda bb, i, n, kc: (0, 0, 0, 0)))
    if vproj is not None:
        ins.append(vproj)
        specs.append(pl.BlockSpec(vproj.shape, lambda bb, i, n, kc: (0, 0, 0)))
    r = hg * tq
    return pl.pallas_call(
        functools.partial(_fa_kernel, hg=hg, dq=dq, dv=dv, tq=tq, tk=tk, nkc=nkc, scale=scale, mode=mode,
                          qpos0=qpos0, kpos0=kpos0, n_chunks=n_chunks, has_mask=mask is not None,
                          has_bias=bias is not None, has_vproj=vproj is not None),
        out_shape=jax.ShapeDtypeStruct((b, tq_total, groups * hg * HEAD_DIM), F32),
        grid=(b, nq, groups, nkc),
        in_specs=specs,
        out_specs=pl.BlockSpec((1, tq, hg * HEAD_DIM), lambda bb, i, n, kc: (bb, i, n)),
        scratch_shapes=[pltpu.VMEM((r, dq), BF16), pltpu.VMEM((r, V7X_LANES), F32), pltpu.VMEM((r, dv + V7X_LANES), F32)],
        compiler_params=_cparams(("parallel", "parallel", "arbitrary", "arbitrary")),
        name=name,
    )(*ins)


def _sortable(x):
    b = lax.bitcast_convert_type(x, I32)
    return b ^ ((b >> 31) & jnp.int32(0x7FFFFFFF))


_NT = (((1,), (1,)), ((), ()))


def _select_scores(qstack, kic, wcols, rows, *, keys_minor=False):
    if keys_minor:
        d = jnp.dot(qstack, kic, preferred_element_type=F32)
    else:
        d = lax.dot_general(qstack, kic, _NT, preferred_element_type=F32)
    acc = None
    for h in range(IDX_HEADS):
        term = jnp.maximum(d[h * rows:(h + 1) * rows], 0.0) * wcols[h]
        acc = term if acc is None else acc + term
    return acc + 0.0


def _select_topk(key_ref, emit, *, rows, tk, nck, n_chunks, n_keep, qpos, live):
    lane = lax.broadcasted_iota(I32, (rows, tk), 1)
    nl = tk // V7X_LANES

    def count(pred):
        def body(kc, c):
            hit = jnp.where(pred(key_ref[kc], kc), 1.0, 0.0)
            part = hit[:, :V7X_LANES]
            for u in range(1, nl):
                part = part + hit[:, u * V7X_LANES:(u + 1) * V7X_LANES]
            return c + part
        total = lax.fori_loop(0, nck, body, jnp.zeros((rows, V7X_LANES), F32), unroll=isinstance(nck, int))
        return jnp.sum(total, axis=1, keepdims=True)

    neg_key = jnp.int32(np.array(NEG, np.float32).view(np.int32) ^ 0x7FFFFFFF)
    extra = jnp.asarray((n_chunks - nck) * tk).astype(F32)
    kf = float(n_keep)

    def count_ge(cand):
        return count(lambda kk, kc: kk >= cand) + jnp.where(cand <= neg_key, extra, 0.0)

    tau = jnp.where(count_ge(jnp.zeros((rows, 1), I32)) >= kf, 0, jnp.iinfo(jnp.int32).min).astype(I32)
    for bit in range(30, -1, -1):
        cand = tau | jnp.int32(1 << bit)
        tau = jnp.where(count_ge(cand) >= kf, cand, tau)
    n_ge = count_ge(tau)
    ties = jnp.max(jnp.where((n_ge != kf) & live, 1.0, 0.0)) > 0.5

    def write(sel_fn):
        def body(kc, carry):
            ok = sel_fn(key_ref[kc], kc) & (kc * tk + lane <= qpos)
            emit(kc, jnp.where(ok, 0.0, NEG))
            return carry
        lax.fori_loop(0, nck, body, 0)

        def fill(kc, carry):
            emit(kc, jnp.full((rows, tk), NEG, F32))
            return carry
        lax.fori_loop(nck, n_chunks, fill, 0)

    @pl.when(jnp.logical_not(ties))
    def _():
        write(lambda kk, kc: kk >= tau)

    @pl.when(ties)
    def _():
        r = kf - count(lambda kk, kc: kk > tau) - jnp.where(tau < neg_key, extra, 0.0)
        nbits = int(math.ceil(math.log2(n_chunks * tk)))
        mth = jnp.zeros((rows, 1), I32)
        for bit in range(nbits - 1, -1, -1):
            cand = mth | jnp.int32(1 << bit)
            c = count(lambda kk, kc: (kk == tau) & (kc * tk + lane < cand))
            c = c + jnp.where(tau == neg_key, jnp.clip(cand - nck * tk, 0, (n_chunks - nck) * tk).astype(F32), 0.0)
            mth = jnp.where(c < r, cand, mth)
        write(lambda kk, kc: (kk > tau) | ((kk == tau) & (kc * tk + lane <= mth)))


def _stack_index_queries(qs_ref, qa, qb, rows):
    hh = IDX_HEADS // 2
    for h in range(IDX_HEADS):
        src = qa if h < hh else qb
        qs_ref[h * rows:(h + 1) * rows, :] = src[:, (h % hh) * IDX_DIM:(h % hh + 1) * IDX_DIM].astype(BF16)


def _dsa_select_kernel(qa_ref, qb_ref, wi_ref, kidx_ref, o_ref, key_ref, qs_ref, *, tq, tk, n_chunks, n_keep):
    i = pl.program_id(1)
    t0 = i * tq
    nck = jnp.minimum((t0 + tq - 1) // tk + 1, n_chunks)
    _stack_index_queries(qs_ref, qa_ref[0], qb_ref[0], tq)
    w = wi_ref[0][:, IDX_DIM:IDX_DIM + IDX_HEADS] * (IDX_HEADS ** -0.5)
    wcols = [w[:, h:h + 1] for h in range(IDX_HEADS)]
    qpos = t0 + lax.broadcasted_iota(I32, (tq, tk), 0)
    lane = lax.broadcasted_iota(I32, (tq, tk), 1)

    def score_chunk(kc, carry):
        kic = kidx_ref[0, pl.ds(pl.multiple_of(kc * tk, tk), tk), :][:, :IDX_DIM].astype(BF16)
        sc = _select_scores(qs_ref[...], kic, wcols, tq)
        key_ref[kc] = _sortable(jnp.where(kc * tk + lane <= qpos, sc, NEG))
        return carry

    lax.fori_loop(0, nck, score_chunk, 0)

    def emit(kc, mask):
        o_ref[0, 0, kc] = mask

    _select_topk(key_ref, emit, rows=tq, tk=tk, nck=nck, n_chunks=n_chunks, n_keep=n_keep, qpos=qpos,
                 live=jnp.full((tq, 1), True))


def _dsa_select(z, *, tq, tk, n_keep):
    b, t = z.shape[0], z.shape[1]
    n_chunks = t // tk
    half = IDX_HEADS * IDX_DIM // 2
    qi_col = (A_HEADS * HEAD_DIM + 2 * A_KV_HEADS * HEAD_DIM) // half
    tail = TAIL_COL // V7X_LANES
    return pl.pallas_call(
        functools.partial(_dsa_select_kernel, tq=tq, tk=tk, n_chunks=n_chunks, n_keep=n_keep),
        out_shape=jax.ShapeDtypeStruct((b, 1, n_chunks, t, tk), F32),
        grid=(b, t // tq),
        in_specs=[
            pl.BlockSpec((1, tq, half), lambda bb, i: (bb, i, qi_col)),
            pl.BlockSpec((1, tq, half), lambda bb, i: (bb, i, qi_col + 1)),
            pl.BlockSpec((1, tq, V7X_LANES), lambda bb, i: (bb, i, tail)),
            pl.BlockSpec((1, t, V7X_LANES), lambda bb, i: (bb, 0, tail)),
        ],
        out_specs=pl.BlockSpec((1, 1, n_chunks, tq, tk), lambda bb, i: (bb, 0, 0, i, 0)),
        scratch_shapes=[pltpu.VMEM((n_chunks, tq, tk), I32), pltpu.VMEM((IDX_HEADS * tq, IDX_DIM), BF16)],
        compiler_params=_cparams(("parallel", "arbitrary")),
        name="dsa_select",
    )(z, z, z, z)


def _page_copies(pt_ref, pool_ref, dst, sem, *, slot, row, n_pages, rows_per_page):
    def at(p):
        return pltpu.make_async_copy(
            pool_ref.at[slot, pt_ref[row, p]],
            dst.at[pl.ds(pl.multiple_of(p * rows_per_page, rows_per_page), rows_per_page)], sem)
    return at


def _dsa_select_paged_kernel(pt_ref, qa_ref, qb_ref, wi_ref, knew_ref, pool_ref, o_ref, kbuf, sem, key_ref, qs_ref, *,
                             slot, tb, tk, n_pages, n_keep, past, n_steps):
    s = pl.program_id(0)
    tq = ROW_TILE_DEC
    rows = tb * tq
    nc_past = past // tk
    n_chunks = nc_past + 1

    ppc = tk // PAGE_SIZE

    def each_page(step, si, fn):
        for j in range(tb):
            def body(c, carry):
                for u in range(ppc):
                    page = pt_ref[step * tb + j, c * ppc + u]
                    fn(pltpu.make_async_copy(pool_ref.at[slot, page], kbuf.at[si, j, c, :, pl.ds(u * PAGE_SIZE, PAGE_SIZE)],
                                             sem.at[si]))
                return carry
            lax.fori_loop(0, nc_past, body, 0)

    @pl.when(s == 0)
    def _():
        each_page(0, 0, lambda c: c.start())

    @pl.when(s + 1 < n_steps)
    def _():
        each_page(s + 1, (s + 1) % 2, lambda c: c.start())

    si = s % 2
    each_page(s, si, lambda c: c.wait())

    lane_new = lax.broadcasted_iota(I32, (tq, PAGE_SIZE), 1)
    arow = lax.broadcasted_iota(I32, (tq, PAGE_SIZE), 0)
    for j in range(tb):
        _stack_index_queries(qs_ref, qa_ref[j], qb_ref[j], tq)
        w = wi_ref[j][:, IDX_DIM:IDX_DIM + IDX_HEADS] * (IDX_HEADS ** -0.5)
        wcols = [w[:, h:h + 1] for h in range(IDX_HEADS)]

        def score_chunk(kc, carry):
            kic = kbuf[si, j, kc].astype(BF16)
            key_ref[kc, j * tq:(j + 1) * tq, :] = _sortable(_select_scores(qs_ref[...], kic, wcols, tq, keys_minor=True))
            return carry

        lax.fori_loop(0, nc_past, score_chunk, 0)
        sc = _select_scores(qs_ref[...], knew_ref[j].astype(BF16), wcols, tq, keys_minor=True)
        sc = jnp.where(lane_new <= arow, sc, NEG)
        key_ref[nc_past, j * tq:(j + 1) * tq, :] = _sortable(
            jnp.concatenate([sc, jnp.full((tq, tk - PAGE_SIZE), NEG, F32)], axis=1))

    a = lax.broadcasted_iota(I32, (rows, tk), 0) % tq
    qpos = past + a

    def emit(kc, mask):
        for j in range(tb):
            o_ref[j, 0, kc] = mask[j * tq:(j + 1) * tq]

    live = lax.broadcasted_iota(I32, (rows, 1), 0) % tq < tq // 2
    _select_topk(key_ref, emit, rows=rows, tk=tk, nck=n_chunks, n_chunks=n_chunks, n_keep=n_keep, qpos=qpos, live=live)


def _dsa_select_paged(z8, knew, pool, page_table, *, slot, tb, tk, n_keep):
    b = z8.shape[0]
    n_pages = page_table.shape[1]
    past = n_pages * PAGE_SIZE
    n_chunks = past // tk + 1
    tq = ROW_TILE_DEC
    half = IDX_HEADS * IDX_DIM // 2
    qi_col = (A_HEADS * HEAD_DIM + 2 * A_KV_HEADS * HEAD_DIM) // half
    tail = TAIL_COL // V7X_LANES
    n_steps = b // tb
    grid_spec = pltpu.PrefetchScalarGridSpec(
        num_scalar_prefetch=1,
        grid=(n_steps,),
        in_specs=[
            pl.BlockSpec((tb, tq, half), lambda s, pt: (s, 0, qi_col)),
            pl.BlockSpec((tb, tq, half), lambda s, pt: (s, 0, qi_col + 1)),
            pl.BlockSpec((tb, tq, V7X_LANES), lambda s, pt: (s, 0, tail)),
            pl.BlockSpec((tb, IDX_DIM, PAGE_SIZE), lambda s, pt: (s, 0, 0)),
            pl.BlockSpec(memory_space=pl.ANY),
        ],
        out_specs=pl.BlockSpec((tb, 1, n_chunks, tq, tk), lambda s, pt: (s, 0, 0, 0, 0)),
        scratch_shapes=[
            pltpu.VMEM((2, tb, past // tk, IDX_DIM, tk), F32),
            pltpu.SemaphoreType.DMA((2,)),
            pltpu.VMEM((n_chunks, tb * tq, tk), I32),
            pltpu.VMEM((IDX_HEADS * tq, IDX_DIM), BF16),
        ],
    )
    return pl.pallas_call(
        functools.partial(_dsa_select_paged_kernel, slot=slot, tb=tb, tk=tk, n_pages=n_pages, n_keep=n_keep, past=past,
                          n_steps=n_steps),
        out_shape=jax.ShapeDtypeStruct((b, 1, n_chunks, tq, tk), F32),
        grid_spec=grid_spec,
        compiler_params=_cparams(("arbitrary",)),
        name="dsa_select_dec",
    )(page_table, z8, z8, z8, knew, pool)


def _paged_attn_kernel(pt_ref, q_ref, knew_ref, vnew_ref, pk_ref, pv_ref, *rest, kind, slot, nc, nb, depth, past, scale,
                       mask_groups, has_bias, has_vproj):
    it = iter(rest)
    mask_ref = next(it) if mask_groups else None
    bias_ref = next(it) if has_bias else None
    wv_ref = next(it) if has_vproj else None
    o_ref, kbuf, vbuf, sem, qs_ref, m_ref, acc_ref = (next(it) for _ in range(7))
    b = pl.program_id(0)
    tq = ROW_TILE_DEC
    groups, hg = (A_KV_HEADS, A_HEADS // A_KV_HEADS) if kind == "gqa" else (1, B_HEADS)
    rpp = PAGE_SIZE * groups
    tk = kbuf.shape[1] // groups
    ppc = tk // PAGE_SIZE
    dv = HEAD_DIM if kind == "gqa" else B_LORA
    total = nb * nc

    def copies(g):
        bb = g // nc
        cc = g - bb * nc
        si = g % depth
        out = []
        for u in range(ppc):
            page = pt_ref[bb, cc * ppc + u]
            rows = pl.ds(u * rpp, rpp)
            out.append(pltpu.make_async_copy(pk_ref.at[slot, page], kbuf.at[si, rows], sem.at[si, 0]))
            if kind == "gqa":
                out.append(pltpu.make_async_copy(pv_ref.at[slot, page], vbuf.at[si, rows], sem.at[si, 1]))
            else:
                out.append(pltpu.make_async_copy(pv_ref.at[slot, page], vbuf.at[si, :, pl.ds(u * PAGE_SIZE, PAGE_SIZE)],
                                                 sem.at[si, 1]))
        return out

    @pl.when(b == 0)
    def _():
        for g in range(depth - 1):
            for c in copies(g):
                c.start()

    for n in range(groups):
        for h in range(hg):
            if kind == "gqa":
                qs_ref[n, h * tq:(h + 1) * tq, :] = q_ref[0, :, (n * hg + h) * HEAD_DIM:(n * hg + h + 1) * HEAD_DIM].astype(BF16)
            else:
                qs_ref[n, h * tq:(h + 1) * tq, :] = q_ref[0, :, h * MLA_QK:(h + 1) * MLA_QK]
    m_ref[...] = jnp.full(m_ref.shape, NEG, F32)
    acc_ref[...] = jnp.zeros(acc_ref.shape, F32)

    def update(n, s, vb):
        keys = s.shape[1]
        m_old = m_ref[n]
        m_new = jnp.maximum(m_old, jnp.max(s, axis=1, keepdims=True))
        alpha = jnp.exp(m_old - m_new)
        p = jnp.exp(s - jnp.concatenate([m_new] * (keys // V7X_LANES), axis=1))
        vext = jnp.concatenate([vb, jnp.ones((keys, V7X_LANES), BF16)], axis=1)
        pv = jnp.dot(p.astype(BF16), vext, preferred_element_type=F32)
        acc_ref[n] = acc_ref[n] * jnp.concatenate([alpha] * (dv // V7X_LANES + 1), axis=1) + pv
        m_ref[n] = m_new

    def attend(kget, vget, add, bias_sel):
        for n in range(groups):
            kb = kget(n).astype(BF16)
            vb = vget(n).astype(BF16)
            if kind == "gqa":
                s = lax.dot_general(qs_ref[n], kb, _NT, preferred_element_type=F32)
                vn = vb
            else:
                q = qs_ref[n]
                s = (lax.dot_general(q[:, :B_LORA], kb, _NT, preferred_element_type=F32)
                     + jnp.dot(q[:, B_LORA:B_LORA + B_ROPE], vb, preferred_element_type=F32))
                vn = kb
            s = s * scale
            if has_bias:
                parts = [bias_ref[n, d] for d in bias_sel]
                s = s + (parts[0] if len(parts) == 1 else jnp.concatenate(parts, axis=1))
            a = add(n)
            if a is not None:
                s = s + jnp.concatenate([a] * hg, axis=0)
            update(n, s, vn)

    def chunk(c, carry):
        g = b * nc + c
        nxt = g + depth - 1

        @pl.when(nxt < total)
        def _():
            for cp in copies(nxt):
                cp.start()

        for cp in copies(g):
            cp.wait()
        si = g % depth
        qtile = past // V7X_LANES
        bias_sel = [jnp.clip(qtile - (c * ppc + u), 0, 2) for u in range(ppc)]
        add = (lambda n: mask_ref[0, n if mask_groups > 1 else 0, c]) if mask_groups else (lambda n: None)
        if kind == "gqa":
            attend(lambda n: kbuf.at[si][pl.ds(n, tk, stride=groups), :], lambda n: vbuf.at[si][pl.ds(n, tk, stride=groups), :],
                   add, bias_sel)
        else:
            attend(lambda n: kbuf[si], lambda n: vbuf[si], add, bias_sel)
        return carry

    lax.fori_loop(0, nc, chunk, 0)

    if mask_groups:
        add_new = lambda n: mask_ref[0, n if mask_groups > 1 else 0, nc][:, :PAGE_SIZE]
    else:
        kp = lax.broadcasted_iota(I32, (tq, PAGE_SIZE), 1)
        qp = lax.broadcasted_iota(I32, (tq, PAGE_SIZE), 0)
        causal = jnp.where(kp <= qp, 0.0, NEG)
        add_new = lambda n: causal
    if kind == "gqa":
        cols = lambda n: slice(n * HEAD_DIM, (n + 1) * HEAD_DIM)
        attend(lambda n: knew_ref[0][:, cols(n)], lambda n: vnew_ref[0][:, cols(n)], add_new, [0])
    else:
        attend(lambda n: knew_ref[0], lambda n: vnew_ref[0], add_new, [0])

    for n in range(groups):
        valid = m_ref[n] > 0.5 * NEG
        acc = acc_ref[n]
        l = jnp.where(valid, acc[:, dv:dv + V7X_LANES], 1.0)
        inv = jnp.where(valid, 1.0 / l, 0.0)
        o = acc[:, :dv] * jnp.concatenate([inv] * (dv // V7X_LANES), axis=1)
        for h in range(hg):
            oh = o[h * tq:(h + 1) * tq, :]
            if has_vproj:
                oh = jnp.dot(oh.astype(BF16), wv_ref[h], preferred_element_type=F32)
            o_ref[0, :, (n * hg + h) * HEAD_DIM:(n * hg + h + 1) * HEAD_DIM] = oh


def _paged_attn(q, knew, vnew, pool_k, pool_v, page_table, *, kind, slot, scale, tk=512, depth=8, mask=None, bias=None, vproj=None,
                name="paged_attn"):
    b = q.shape[0]
    n_pages = page_table.shape[1]
    nc = n_pages * PAGE_SIZE // tk
    past = n_pages * PAGE_SIZE
    assert b * nc >= depth
    tq = ROW_TILE_DEC
    if kind == "gqa":
        groups, hg, dq, dv, qw = A_KV_HEADS, A_HEADS // A_KV_HEADS, HEAD_DIM, HEAD_DIM, A_HEADS * HEAD_DIM
    else:
        groups, hg, dq, dv, qw = 1, B_HEADS, MLA_QK, B_LORA, B_HEADS * MLA_QK
    wk = pool_k.shape[-1]
    assert pool_k.shape[2] == PAGE_SIZE * groups
    if kind == "gqa":
        assert pool_v.shape[2:] == (PAGE_SIZE * groups, HEAD_DIM)
        vbuf_shape = (depth, tk * groups, HEAD_DIM)
    else:
        assert pool_v.shape[2:] == (B_ROPE, PAGE_SIZE)
        vbuf_shape = (depth, B_ROPE, tk)
    ins = [page_table, q, knew, vnew, pool_k, pool_v]
    specs = [
        pl.BlockSpec((1, tq, qw), lambda i, pt: (i, 0, 0)),
        pl.BlockSpec((1,) + knew.shape[1:], lambda i, pt: (i, 0, 0)),
        pl.BlockSpec((1,) + vnew.shape[1:], lambda i, pt: (i, 0, 0)),
        pl.BlockSpec(memory_space=pl.ANY),
        pl.BlockSpec(memory_space=pl.ANY),
    ]
    mask_groups = 0
    if mask is not None:
        mask_groups = mask.shape[1]
        ins.append(mask)
        specs.append(pl.BlockSpec((1,) + mask.shape[1:], lambda i, pt: (i, 0, 0, 0, 0)))
    if bias is not None:
        ins.append(bias)
        specs.append(pl.BlockSpec(bias.shape, lambda i, pt: (0, 0, 0, 0)))
    if vproj is not None:
        ins.append(vproj)
        specs.append(pl.BlockSpec(vproj.shape, lambda i, pt: (0, 0, 0)))
    r = hg * tq
    grid_spec = pltpu.PrefetchScalarGridSpec(
        num_scalar_prefetch=1,
        grid=(b,),
        in_specs=specs,
        out_specs=pl.BlockSpec((1, tq, groups * hg * HEAD_DIM), lambda i, pt: (i, 0, 0)),
        scratch_shapes=[
            pltpu.VMEM((depth, tk * groups, wk), F32),
            pltpu.VMEM(vbuf_shape, F32),
            pltpu.SemaphoreType.DMA((depth, 2)),
            pltpu.VMEM((groups, r, dq), BF16),
            pltpu.VMEM((groups, r, V7X_LANES), F32),
            pltpu.VMEM((groups, r, dv + V7X_LANES), F32),
        ],
    )
    return pl.pallas_call(
        functools.partial(_paged_attn_kernel, kind=kind, slot=slot, nc=nc, nb=b, depth=depth, past=past, scale=scale,
                          mask_groups=mask_groups, has_bias=bias is not None, has_vproj=vproj is not None),
        out_shape=jax.ShapeDtypeStruct((b, tq, groups * hg * HEAD_DIM), F32),
        grid_spec=grid_spec,
        compiler_params=_cparams(("arbitrary",)),
        name=name,
    )(*ins)


def _rope_rows(x, cos, sin):
    half = x.shape[-1] // 2
    x1, x2 = x[:, :half], x[:, half:]
    return jnp.concatenate([x1 * cos - x2 * sin, x1 * sin + x2 * cos], axis=-1)


def _mla_prep_kernel(qf_ref, c_ref, kr_ref, gk_ref, wuk_ref, cos_ref, sin_ref, qcat_ref, ckr_ref, cout_ref, krout_ref):
    cos, sin = cos_ref[...], sin_ref[...]
    tq = cos.shape[0]
    c = c_ref[0]
    cn = c * lax.rsqrt(jnp.mean(c * c, axis=-1, keepdims=True) + RMS_EPS) * gk_ref[...]
    kr = _rope_rows(kr_ref[0][:, :B_ROPE], cos, sin)
    cout_ref[0] = cn
    krout_ref[0] = kr
    zpad = jnp.zeros((tq, MLA_QK - B_LORA - B_ROPE), BF16)
    ckr_ref[0] = jnp.concatenate([cn.astype(BF16), kr.astype(BF16), zpad], axis=1)
    hd = B_NOPE + B_ROPE
    for h in range(B_HEADS):
        qn = qf_ref[0, :, h * hd:h * hd + B_NOPE].astype(BF16)
        qr = _rope_rows(qf_ref[0, :, h * hd + B_NOPE:(h + 1) * hd], cos, sin)
        ql = jnp.dot(qn, wuk_ref[h], preferred_element_type=F32)
        qcat_ref[0, :, h * MLA_QK:(h + 1) * MLA_QK] = jnp.concatenate([ql.astype(BF16), qr.astype(BF16), zpad], axis=1)


def _mla_prep(z, kv_norm, wuk_t, cos, sin, *, tq):
    b, t = z.shape[0], z.shape[1]
    qw = B_HEADS * (B_NOPE + B_ROPE)
    half = B_ROPE // 2
    return pl.pallas_call(
        _mla_prep_kernel,
        out_shape=(
            jax.ShapeDtypeStruct((b, t, B_HEADS * MLA_QK), BF16),
            jax.ShapeDtypeStruct((b, t, MLA_QK), BF16),
            jax.ShapeDtypeStruct((b, t, B_LORA), F32),
            jax.ShapeDtypeStruct((b, t, B_ROPE), F32),
        ),
        grid=(b, t // tq),
        in_specs=[
            pl.BlockSpec((1, tq, qw), lambda bb, i: (bb, i, 0)),
            pl.BlockSpec((1, tq, B_LORA), lambda bb, i: (bb, i, qw // B_LORA)),
            pl.BlockSpec((1, tq, V7X_LANES), lambda bb, i: (bb, i, TAIL_COL // V7X_LANES)),
            pl.BlockSpec((1, B_LORA), lambda bb, i: (0, 0)),
            pl.BlockSpec(wuk_t.shape, lambda bb, i: (0, 0, 0)),
            pl.BlockSpec((tq, half), lambda bb, i: (i, 0)),
            pl.BlockSpec((tq, half), lambda bb, i: (i, 0)),
        ],
        out_specs=(
            pl.BlockSpec((1, tq, B_HEADS * MLA_QK), lambda bb, i: (bb, i, 0)),
            pl.BlockSpec((1, tq, MLA_QK), lambda bb, i: (bb, i, 0)),
            pl.BlockSpec((1, tq, B_LORA), lambda bb, i: (bb, i, 0)),
            pl.BlockSpec((1, tq, B_ROPE), lambda bb, i: (bb, i, 0)),
        ),
        compiler_params=_cparams(("parallel", "parallel")),
        name="mla_prep",
    )(z, z, z, kv_norm.reshape(1, B_LORA), wuk_t, cos, sin)


def _compress_lines(x, w1_ref, w2_ref, o_ref, *, n_sub, n_out):
    gw = C_GROUPS * HEAD_DIM
    first = jnp.zeros((n_sub, gw), F32)
    second = jnp.zeros((n_sub, gw), F32)
    for r in range(CMP_STRIDE):
        xr = x(r)
        first = first + xr * w1_ref[:, r * gw:(r + 1) * gw]
        second = second + xr * w2_ref[:, r * gw:(r + 1) * gw]
    blocks = (first + pltpu.roll(second, n_sub - 1, axis=0)) * (1.0 / CMP_LEN)
    o_ref[0, :n_sub, :] = blocks
    if n_out > n_sub:
        o_ref[0, n_sub:, :] = jnp.zeros((n_out - n_sub, gw), F32)


def _nsa_compress_kernel(xk_ref, xv_ref, wk1_ref, wk2_ref, wv1_ref, wv2_ref, ok_ref, ov_ref, *, n_sub, n_out):
    gw = C_GROUPS * HEAD_DIM
    for x_ref, w1_ref, w2_ref, o_ref in ((xk_ref, wk1_ref, wk2_ref, ok_ref), (xv_ref, wv1_ref, wv2_ref, ov_ref)):
        _compress_lines(lambda r, x_ref=x_ref: x_ref[0, :, r * gw:(r + 1) * gw], w1_ref, w2_ref, o_ref, n_sub=n_sub, n_out=n_out)


def _flat_cmp_weights(w):
    width = CMP_STRIDE * C_GROUPS * HEAD_DIM
    w2 = jnp.broadcast_to(w.reshape(2, CMP_STRIDE, 1, HEAD_DIM), (2, CMP_STRIDE, C_GROUPS, HEAD_DIM))
    w2 = w2.reshape(2, 1, width)
    return w2[0], w2[1]


LINE_ROWS = CMP_STRIDE * C_GROUPS


def _nsa_compress_paged_kernel(pt_ref, nk_ref, nv_ref, pk_ref, pv_ref, wk_ref, wv_ref, ok_ref, ov_ref,
                               xk, xv, sem, t1_ref, t2_ref, *, slot, n_pages, n_sub, n_out, n_steps, n_parts):
    s = pl.program_id(0)
    rpp = PAGE_SIZE * C_GROUPS
    tiles = LINE_ROWS // V7X_SUBLANES
    sub = V7X_SUBLANES

    def each_page(step, si, fn):
        for a, (pool, buf) in enumerate(((pk_ref, xk), (pv_ref, xv))):
            cp = _page_copies(pt_ref, pool, buf.at[si], sem.at[si, a], slot=slot, row=step, n_pages=n_pages, rows_per_page=rpp)

            def body(p, carry):
                fn(cp(p))
                return carry
            lax.fori_loop(0, n_pages, body, 0)

    @pl.when(s == 0)
    def _():
        each_page(0, 0, lambda c: c.start())

    @pl.when(s + 1 < n_steps)
    def _():
        each_page(s + 1, (s + 1) % 2, lambda c: c.start())

    si = s % 2
    each_page(s, si, lambda c: c.wait())
    paged = n_pages * rpp
    lines = n_sub // n_parts
    for buf, n_ref, w_ref, o_ref in ((xk, nk_ref, wk_ref, ok_ref), (xv, nv_ref, wv_ref, ov_ref)):
        buf[si, paged:, :] = n_ref[0]
        w1 = w_ref[0].reshape(1, tiles, sub, HEAD_DIM)
        w2 = w_ref[1].reshape(1, tiles, sub, HEAD_DIM)
        for part in range(n_parts):
            x = buf[si, part * lines * LINE_ROWS:(part + 1) * lines * LINE_ROWS, :].reshape(lines, tiles, sub, HEAD_DIM)
            rows = slice(part * lines * sub, (part + 1) * lines * sub)
            t1_ref[rows, :] = jnp.sum(x * w1, axis=1).reshape(lines * sub, HEAD_DIM)
            t2_ref[rows, :] = jnp.sum(x * w2, axis=1).reshape(lines * sub, HEAD_DIM)
        for g in range(C_GROUPS):
            first = second = None
            for j in range(sub // C_GROUPS):
                r = pl.ds(C_GROUPS * j + g, n_sub, stride=sub)
                first = t1_ref[r, :] if first is None else first + t1_ref[r, :]
                second = t2_ref[r, :] if second is None else second + t2_ref[r, :]
            blocks = (first + pltpu.roll(second, n_sub - 1, axis=0)) * (1.0 / CMP_LEN)
            o_ref[0, :n_sub, g * HEAD_DIM:(g + 1) * HEAD_DIM] = blocks
        if n_out > n_sub:
            o_ref[0, n_sub:, :] = jnp.zeros((n_out - n_sub, C_GROUPS * HEAD_DIM), F32)


def _nsa_compress_paged(new_k, new_v, pool_k, pool_v, page_table, wk, wv, *, slot, n_out):
    b = new_k.shape[0]
    n_pages = page_table.shape[1]
    rpp = PAGE_SIZE * C_GROUPS
    n_rows = n_pages * rpp + new_k.shape[1]
    n_sub = n_rows // LINE_ROWS
    n_parts = 4 if n_sub % 4 == 0 else 1
    tile_w = lambda w: jnp.repeat(w.reshape(2, CMP_STRIDE, HEAD_DIM), C_GROUPS, axis=1)
    wspec = pl.BlockSpec((2, LINE_ROWS, HEAD_DIM), lambda s, pt: (0, 0, 0))
    nspec = pl.BlockSpec((1, new_k.shape[1], HEAD_DIM), lambda s, pt: (s, 0, 0))
    ospec = pl.BlockSpec((1, n_out, C_GROUPS * HEAD_DIM), lambda s, pt: (s, 0, 0))
    grid_spec = pltpu.PrefetchScalarGridSpec(
        num_scalar_prefetch=1,
        grid=(b,),
        in_specs=[nspec, nspec, pl.BlockSpec(memory_space=pl.ANY), pl.BlockSpec(memory_space=pl.ANY), wspec, wspec],
        out_specs=(ospec, ospec),
        scratch_shapes=[pltpu.VMEM((2, n_rows, HEAD_DIM), F32), pltpu.VMEM((2, n_rows, HEAD_DIM), F32),
                        pltpu.SemaphoreType.DMA((2, 2)),
                        pltpu.VMEM((n_sub * V7X_SUBLANES, HEAD_DIM), F32), pltpu.VMEM((n_sub * V7X_SUBLANES, HEAD_DIM), F32)],
    )
    return pl.pallas_call(
        functools.partial(_nsa_compress_paged_kernel, slot=slot, n_pages=n_pages, n_sub=n_sub, n_out=n_out, n_steps=b,
                          n_parts=n_parts),
        out_shape=(jax.ShapeDtypeStruct((b, n_out, C_GROUPS * HEAD_DIM), F32),) * 2,
        grid_spec=grid_spec,
        compiler_params=_cparams(("arbitrary",)),
        name="nsa_compress_dec",
    )(page_table, new_k, new_v, pool_k, pool_v, tile_w(wk), tile_w(wv))


def _nsa_compress(xk, xv, wk, wv, *, n_out):
    b, n_sub, width = xk.shape
    wk1, wk2 = _flat_cmp_weights(wk)
    wv1, wv2 = _flat_cmp_weights(wv)
    wspec = pl.BlockSpec((1, width), lambda bb: (0, 0))
    return pl.pallas_call(
        functools.partial(_nsa_compress_kernel, n_sub=n_sub, n_out=n_out),
        out_shape=(jax.ShapeDtypeStruct((b, n_out, C_GROUPS * HEAD_DIM), F32),) * 2,
        grid=(b,),
        in_specs=[pl.BlockSpec((1, n_sub, width), lambda bb: (bb, 0, 0))] * 2 + [wspec] * 4,
        out_specs=(pl.BlockSpec((1, n_out, C_GROUPS * HEAD_DIM), lambda bb: (bb, 0, 0)),) * 2,
        compiler_params=_cparams(("parallel",)),
        name="nsa_compress",
    )(xk, xv, wk1, wk2, wv1, wv2)


def _nsa_cmp_kernel(q_ref, kcb_ref, vcb_ref, bias_ref, m_ref, e_ref, o_ref, mask_ref, *, tq, tk, nb, ns_pad, n_slc, n_top, tkeys, qpos0):
    i = pl.program_id(1)
    t0 = qpos0 + i * tq
    hg = C_HEADS // C_GROUPS
    scale = HEAD_DIM ** -0.5
    r = hg * tq
    qrow = t0 + lax.broadcasted_iota(I32, (tq, nb), 0)
    cend = lax.broadcasted_iota(I32, (tq, nb), 1) * CMP_STRIDE + (CMP_LEN - 1)
    vis = jnp.where(cend <= qrow, 0.0, NEG)
    qs = t0 + lax.broadcasted_iota(I32, (tq, ns_pad), 0)
    blk = lax.broadcasted_iota(I32, (tq, ns_pad), 1)
    forced = (blk == qs // SLC_LEN) | (blk == 0)
    admissible = blk * SLC_LEN <= qs
    qk = t0 + lax.broadcasted_iota(I32, (tq, tkeys), 0)
    kpos = lax.broadcasted_iota(I32, (tq, tkeys), 1)
    for g in range(C_GROUPS):
        qg = jnp.concatenate([q_ref[0, :, (g * hg + h) * HEAD_DIM:(g * hg + h + 1) * HEAD_DIM] for h in range(hg)],
                             axis=0).astype(BF16)
        kg = kcb_ref[0, :, g * HEAD_DIM:(g + 1) * HEAD_DIM].astype(BF16)
        vg = vcb_ref[0, :, g * HEAD_DIM:(g + 1) * HEAD_DIM].astype(BF16)
        s = lax.dot_general(qg, kg, (((1,), (1,)), ((), ())), preferred_element_type=F32) * scale
        s = s + bias_ref[g, 0] + jnp.concatenate([vis] * hg, axis=0)
        mx = jnp.max(s, axis=1, keepdims=True)
        e = jnp.exp(s - mx)
        valid = mx > 0.5 * NEG
        p = e * jnp.where(valid, 1.0 / jnp.sum(e, axis=1, keepdims=True), 0.0)
        o = jnp.dot(p.astype(BF16), vg, preferred_element_type=F32)
        psum = p[0:tq]
        for h in range(hg):
            o_ref[0, :, (g * hg + h) * HEAD_DIM:(g * hg + h + 1) * HEAD_DIM] = o[h * tq:(h + 1) * tq]
            if h:
                psum = psum + p[h * tq:(h + 1) * tq]
        p_hi = psum.astype(BF16)
        p_lo = (psum - p_hi.astype(F32)).astype(BF16)
        imp = jnp.dot(p_hi, m_ref[...], preferred_element_type=F32) + jnp.dot(p_lo, m_ref[...], preferred_element_type=F32)
        imp = jnp.where(forced, FORCE_SCORE, jnp.where(admissible, imp, NEG))
        imp = jnp.where(blk < n_slc, imp, MINF)
        rank = jnp.zeros((tq, ns_pad), F32)
        for c in range(n_slc):
            col = imp[:, c:c + 1]
            rank = rank + jnp.where((col > imp) | ((col == imp) & (blk > c)), 1.0, 0.0)
        sel = jnp.where(rank < n_top, 1.0, 0.0).astype(BF16)
        hit = jnp.dot(sel, e_ref[...], preferred_element_type=F32)
        msk = jnp.where((hit > 0.5) & (kpos <= qk), 0.0, NEG)
        for c in range(tkeys // tk):
            mask_ref[0, g, c] = msk[:, c * tk:(c + 1) * tk]


def _nsa_cmp(q, kcb, vcb, bias_c, *, tq, tk, n_slc, tkeys, qpos0, q_col=0):
    b, tq_total = q.shape[0], q.shape[1]
    nb = kcb.shape[1]
    ns_pad = _round_up(n_slc, V7X_LANES)
    c0 = np.arange(nb)[:, None] * CMP_STRIDE
    s0 = np.arange(ns_pad)[None, :] * SLC_LEN
    shared = np.minimum(c0 + CMP_LEN, s0 + SLC_LEN) - np.maximum(c0, s0)
    m = np.maximum(shared, 0).astype(np.float32) / CMP_LEN
    m[:, n_slc:] = 0.0
    e = (np.arange(tkeys)[None, :] // SLC_LEN == np.arange(ns_pad)[:, None]).astype(np.float32)
    hq = C_HEADS * HEAD_DIM
    return pl.pallas_call(
        functools.partial(_nsa_cmp_kernel, tq=tq, tk=tk, nb=nb, ns_pad=ns_pad, n_slc=n_slc, n_top=min(SLC_TOPN, n_slc),
                          tkeys=tkeys, qpos0=qpos0),
        out_shape=(jax.ShapeDtypeStruct((b, tq_total, hq), F32), jax.ShapeDtypeStruct((b, C_GROUPS, tkeys // tk, tq_total, tk), F32)),
        grid=(b, tq_total // tq),
        in_specs=[
            pl.BlockSpec((1, tq, hq), lambda bb, i: (bb, i, q_col)),
            pl.BlockSpec((1, nb, C_GROUPS * HEAD_DIM), lambda bb, i: (bb, 0, 0)),
            pl.BlockSpec((1, nb, C_GROUPS * HEAD_DIM), lambda bb, i: (bb, 0, 0)),
            pl.BlockSpec((C_GROUPS, 1, (C_HEADS // C_GROUPS) * tq, nb), lambda bb, i: (0, i, 0, 0)),
            pl.BlockSpec((nb, ns_pad), lambda bb, i: (0, 0)),
            pl.BlockSpec((ns_pad, tkeys), lambda bb, i: (0, 0)),
        ],
        out_specs=(
            pl.BlockSpec((1, tq, hq), lambda bb, i: (bb, i, 0)),
            pl.BlockSpec((1, C_GROUPS, tkeys // tk, tq, tk), lambda bb, i: (bb, 0, 0, i, 0)),
        ),
        compiler_params=_cparams(("parallel", "parallel")),
        name="nsa_cmp",
    )(q, kcb, vcb, bias_c, jnp.asarray(m, BF16), jnp.asarray(e, BF16))


def _nsa_combine_kernel(oc_ref, os_ref, ow_ref, g_ref, gb_ref, o_ref):
    g = g_ref[...][:, :3 * C_HEADS] + gb_ref[...]
    gate = 1.0 / (1.0 + jnp.exp(-g))
    for h in range(C_HEADS):
        sl = slice(h * HEAD_DIM, (h + 1) * HEAD_DIM)
        o_ref[:, sl] = (oc_ref[:, sl] * gate[:, 3 * h:3 * h + 1] + os_ref[:, sl] * gate[:, 3 * h + 1:3 * h + 2]
                        + ow_ref[:, sl] * gate[:, 3 * h + 2:3 * h + 3])


def _nsa_combine(o_cmp, o_slc, o_win, z, gate_b):
    m, d = o_cmp.shape
    tm = min(m, 512)
    ospec = pl.BlockSpec((tm, d), lambda i: (i, 0))
    return pl.pallas_call(
        _nsa_combine_kernel,
        out_shape=jax.ShapeDtypeStruct((m, d), F32),
        grid=(m // tm,),
        in_specs=[ospec, ospec, ospec, pl.BlockSpec((tm, V7X_LANES), lambda i: (i, TAIL_COL // V7X_LANES)),
                  pl.BlockSpec((1, 3 * C_HEADS), lambda i: (0, 0))],
        out_specs=ospec,
        compiler_params=_cparams(("parallel",)),
        name="nsa_combine",
    )(o_cmp, o_slc, o_win, z, gate_b.reshape(1, 3 * C_HEADS))


def _t5_bucket(dist):
    n = jnp.maximum(dist, 0)
    exact = N_BUCKETS // 2
    nf = jnp.maximum(n, 1).astype(F32)
    large = exact + (jnp.log(nf / exact) / math.log(MAX_DISTANCE / exact) * (N_BUCKETS - exact)).astype(I32)
    return jnp.where(n < exact, n, jnp.minimum(large, N_BUCKETS - 1))


def _bias_by_distance(rel_bias):
    return rel_bias[_t5_bucket(jnp.arange(2 * V7X_LANES))]


def _bias_tiles(f, tq, groups):
    a = jnp.arange(tq)[:, None]
    c = jnp.arange(V7X_LANES)[None, :]
    t0 = f[jnp.clip(a - c, 0, 2 * V7X_LANES - 1)]
    t1 = f[V7X_LANES + a - c]
    t2 = jnp.broadcast_to(f[2 * V7X_LANES - 1], t0.shape)
    tiles = jnp.stack([t0, t1, t2], axis=0)
    heads = f.shape[1]
    tiles = tiles.transpose(3, 0, 1, 2).reshape(groups, heads // groups, 3, tq, V7X_LANES)
    return tiles.transpose(0, 2, 1, 3, 4).reshape(groups, 3, (heads // groups) * tq, V7X_LANES)


def _bias_cmp(f, qpos0, nq, nb, tq, groups):
    heads = f.shape[1]
    hg = heads // groups
    shift = tq // CMP_STRIDE if nq > 1 else 0
    assert nq == 1 or tq % CMP_STRIDE == 0
    lead = shift * (nq - 1)
    a = jnp.arange(tq)[:, None]
    c = jnp.arange(nb + lead)[None, :] - lead
    g0 = f[jnp.clip(qpos0 + a - (c * CMP_STRIDE + CMP_LEN - 1), 0, 2 * V7X_LANES - 1)]
    g0 = g0.reshape(tq, nb + lead, groups, hg).transpose(2, 3, 0, 1).reshape(groups, hg * tq, nb + lead)
    return jnp.stack([g0[:, :, lead - shift * i:lead - shift * i + nb] for i in range(nq)], axis=1)


def _rope_tables(pos):
    half = B_ROPE // 2
    inv = ROPE_THETA ** (-jnp.arange(half, dtype=F32) / half)
    ang = pos.astype(F32)[:, None] * inv[None, :]
    return jnp.cos(ang), jnp.sin(ang)


def _pad_cols(w):
    return jnp.pad(w, ((0, 0), (0, W_IN_PAD - w.shape[1])))


def _pad_rows(x, rows):
    return jnp.pad(x, ((0, 0), (0, rows - x.shape[1])) + ((0, 0),) * (x.ndim - 2))


TK = 512
SELECT_BATCH_DEC = 4


def _positions_minor(x):
    return jnp.swapaxes(x, -1, -2)


def _merge_groups(pool):
    return pool.reshape(pool.shape[:2] + (-1, pool.shape[-1]))


def _mixer_a(z, f_bias, *, decode, caches=None, slot=0, page_table=None):
    b, tq_total = z.shape[0], z.shape[1]
    kcol = A_HEADS * HEAD_DIM
    k_new = z[:, :, kcol:kcol + A_KV_HEADS * HEAD_DIM]
    v_new = z[:, :, kcol + A_KV_HEADS * HEAD_DIM:kcol + 2 * A_KV_HEADS * HEAD_DIM]
    ki_new = z[:, :, TAIL_COL:TAIL_COL + IDX_DIM]
    hg = A_HEADS // A_KV_HEADS
    if not decode:
        tq = V7X_LANES
        t = tq_total
        n_chunks = t // TK
        mask = _dsa_select(z, tq=tq, tk=TK, n_keep=min(IDX_TOPK, t // 4))
        bias = _bias_tiles(f_bias, tq, A_KV_HEADS)
        o = _fa(z, z, z, groups=A_KV_HEADS, hg=hg, dq=HEAD_DIM, dv=HEAD_DIM, tq=tq, tk=TK, nkc=n_chunks,
                n_chunks=n_chunks, scale=HEAD_DIM ** -0.5, mode="mask",
                k_col=lambda n: kcol // HEAD_DIM + n, v_col=lambda n: kcol // HEAD_DIM + A_KV_HEADS + n,
                mask=mask, bias=bias, name="dsa_attn")
    else:
        cache_k, cache_v, cache_idx = caches
        past = page_table.shape[1] * PAGE_SIZE
        tq = ROW_TILE_DEC
        z8 = _pad_rows(z, tq)
        mask = _dsa_select_paged(z8, _positions_minor(_pad_rows(ki_new, PAGE_SIZE)), _positions_minor(cache_idx), page_table,
                                 slot=slot, tb=SELECT_BATCH_DEC, tk=TK, n_keep=min(IDX_TOPK, (past + tq_total) // 4))
        bias = _bias_tiles(f_bias, tq, A_KV_HEADS)
        o = _paged_attn(z8, _pad_rows(k_new, PAGE_SIZE), _pad_rows(v_new, PAGE_SIZE), _merge_groups(cache_k), _merge_groups(cache_v),
                        page_table, kind="gqa", slot=slot, scale=HEAD_DIM ** -0.5, tk=TK, mask=mask, bias=bias, name="dsa_attn_dec")
        o = o[:, :tq_total]
    return o.reshape(b * tq_total, A_HEADS * HEAD_DIM), (k_new, v_new, ki_new)


def _mixer_b(z, kv_norm, w_kvb, *, decode, caches=None, slot=0, page_table=None):
    b, tq_total = z.shape[0], z.shape[1]
    w = w_kvb.reshape(B_LORA, B_HEADS, B_NOPE + B_V)
    wuk_t = w[..., :B_NOPE].transpose(1, 2, 0).astype(BF16)
    wuv = w[..., B_NOPE:].transpose(1, 0, 2).astype(BF16)
    scale = (B_NOPE + B_ROPE) ** -0.5
    if not decode:
        tq = V7X_LANES
        cos, sin = _rope_tables(jnp.arange(tq_total))
        qcat, ckr, c_new, kr_new = _mla_prep(z, kv_norm, wuk_t, cos, sin, tq=tq)
        n_chunks = tq_total // TK
        o = _fa(qcat, ckr, ckr, groups=1, hg=B_HEADS, dq=MLA_QK, dv=B_LORA, tq=tq, tk=TK, nkc=n_chunks, n_chunks=n_chunks,
                scale=scale, mode="causal", q_col=lambda n: 0, k_col=lambda n: 0, v_col=lambda n: 0, vproj=wuv, name="mla_attn")
    else:
        cache_lat, cache_rope = caches
        past = page_table.shape[1] * PAGE_SIZE
        tq = ROW_TILE_DEC
        z8 = _pad_rows(z, tq)
        cos, sin = _rope_tables(past + jnp.arange(tq))
        qcat, ckr8, c8, kr8 = _mla_prep(z8, kv_norm, wuk_t, cos, sin, tq=tq)
        c_new, kr_new = c8[:, :tq_total], kr8[:, :tq_total]
        o = _paged_attn(qcat, _pad_rows(c_new, PAGE_SIZE), _positions_minor(_pad_rows(kr_new, PAGE_SIZE)), cache_lat,
                        _positions_minor(cache_rope), page_table, kind="mla", slot=slot, scale=scale, tk=TK, vproj=wuv,
                        name="mla_attn_dec")
        o = o[:, :tq_total]
    return o.reshape(b * tq_total, B_HEADS * B_V), (c_new, kr_new)


def _mixer_c(z, gate_b, wk_pos, wv_pos, f_bias, *, decode, caches=None, slot=0, win=None, page_table=None):
    b, tq_total = z.shape[0], z.shape[1]
    gw = C_GROUPS * HEAD_DIM
    q0 = C_HEADS * HEAD_DIM
    kc, vc, ks, vs, kw, vw = (z[:, :, q0 + j * gw:q0 + (j + 1) * gw] for j in range(6))
    hg = C_HEADS // C_GROUPS
    scale = HEAD_DIM ** -0.5
    line = CMP_STRIDE * gw
    if not decode:
        t = tq_total
        tq = V7X_LANES
        n_sub = t // CMP_STRIDE
        kcb, vcb = _nsa_compress(kc.reshape(b, n_sub, line), vc.reshape(b, n_sub, line), wk_pos, wv_pos, n_out=n_sub)
        n_slc = -(-t // SLC_LEN)
        bias_c = _bias_cmp(f_bias, 0, t // tq, n_sub, tq, C_GROUPS)
        o_cmp, mask = _nsa_cmp(z, kcb, vcb, bias_c, tq=tq, tk=TK, n_slc=n_slc, tkeys=t, qpos0=0)
        bias = _bias_tiles(f_bias, tq, C_GROUPS)
        n_chunks = t // TK
        blk = lambda j: (lambda n: (q0 + j * gw) // HEAD_DIM + n)
        o_slc = _fa(z, z, z, groups=C_GROUPS, hg=hg, dq=HEAD_DIM, dv=HEAD_DIM, tq=tq, tk=TK, nkc=n_chunks, n_chunks=n_chunks,
                    scale=scale, mode="mask", k_col=blk(2), v_col=blk(3), mask=mask, bias=bias, name="nsa_slc")
        tkw = 2 * V7X_LANES
        o_win = _fa(z, z, z, groups=C_GROUPS, hg=hg, dq=HEAD_DIM, dv=HEAD_DIM, tq=tq, tk=tkw, nkc=WINDOW // tkw + 1,
                    n_chunks=t // tkw, scale=scale, mode="window", k_col=blk(4), v_col=blk(5), bias=bias, name="nsa_win")
        n_w = min(WINDOW, t)
        win_new = (kw[:, t - n_w:], vw[:, t - n_w:])
        zrows = z.reshape(b * t, W_IN_PAD)
    else:
        cache_kc, cache_vc, cache_ks, cache_vs = caches
        win_k, win_v = win
        past = page_table.shape[1] * PAGE_SIZE
        tq = ROW_TILE_DEC
        z8 = _pad_rows(z, tq)
        n_sub = past // CMP_STRIDE + V7X_SUBLANES
        new_rows = lambda x: _pad_rows(x.reshape(b, tq_total * C_GROUPS, HEAD_DIM), V7X_SUBLANES * LINE_ROWS)
        nb = _round_up(n_sub, V7X_LANES)
        kcb, vcb = _nsa_compress_paged(new_rows(kc), new_rows(vc), _merge_groups(cache_kc), _merge_groups(cache_vc), page_table,
                                       wk_pos, wv_pos, slot=slot, n_out=nb)
        n_slc = -(-(past + tq_total) // SLC_LEN)
        tkeys = past + TK
        bias_c = _bias_cmp(f_bias, past, 1, nb, tq, C_GROUPS)
        o_cmp, mask = _nsa_cmp(z8, kcb, vcb, bias_c, tq=tq, tk=TK, n_slc=n_slc, tkeys=tkeys, qpos0=past)
        bias = _bias_tiles(f_bias, tq, C_GROUPS)
        o_slc = _paged_attn(z8, _pad_rows(ks, PAGE_SIZE), _pad_rows(vs, PAGE_SIZE), _merge_groups(cache_ks), _merge_groups(cache_vs),
                            page_table, kind="gqa", slot=slot, scale=scale, tk=TK, mask=mask, bias=bias, name="nsa_slc_dec")
        n_buf = win_k.shape[1]
        wrows = _round_up(n_buf + tq, V7X_LANES)
        tkw = wrows
        kw_cat = jnp.concatenate([win_k.reshape(b, n_buf, gw), kw], axis=1)
        vw_cat = jnp.concatenate([win_v.reshape(b, n_buf, gw), vw], axis=1)
        o_win = _fa(z8, _pad_rows(kw_cat, wrows), _pad_rows(vw_cat, wrows), groups=C_GROUPS, hg=hg, dq=HEAD_DIM, dv=HEAD_DIM,
                    tq=tq, tk=tkw, nkc=wrows // tkw, n_chunks=wrows // tkw, scale=scale, mode="window", qpos0=past,
                    kpos0=past - n_buf, bias=bias, name="nsa_win_dec")
        n_w = min(WINDOW, n_buf + tq_total)
        win_new = (kw_cat[:, -n_w:], vw_cat[:, -n_w:])
        o_cmp, o_slc, o_win = (x[:, :tq_total] for x in (o_cmp, o_slc, o_win))
        zrows = z.reshape(b * tq_total, W_IN_PAD)
    hq = C_HEADS * HEAD_DIM
    o = _nsa_combine(o_cmp.reshape(-1, hq), o_slc.reshape(-1, hq), o_win.reshape(-1, hq), zrows, gate_b)
    return o, (kc, vc, ks, vs) + win_new


def kernel(x_prompt, x_sample, cache_a_k, cache_a_v, cache_a_idx, cache_b_latent, cache_b_rope, cache_c_cmp_k, cache_c_cmp_v, cache_c_slc_k, cache_c_slc_v, state_c_win_k, state_c_win_v, state_ffn_conv, page_table, rel_bias, attn_norm, ffn_norm, final_norm, a_w_in, a_w_o, b_w_in, b_kv_norm, b_w_kvb, b_w_o, c_w_in, c_gate_b, c_cmp_wk, c_cmp_wv, c_w_o, ffn_w_up, ffn_conv_w, ffn_conv_b, ffn_w_down):
    bp, t, d = x_prompt.shape
    bs, ts, _ = x_sample.shape
    hp = x_prompt.reshape(bp * t, d)
    hs = x_sample.reshape(bs * ts, d)
    f_bias = _bias_by_distance(rel_bias)
    st_p = {"a": [], "b": [], "c": [], "conv": []}
    st_s = {"a": [], "b": [], "c": [], "conv": []}
    for i in range(DEPTH):
        kind, slot = i % 3, i // 3
        w_in = _pad_cols((a_w_in, b_w_in, c_w_in)[kind][slot])
        w_o = (a_w_o, b_w_o, c_w_o)[kind]
        zp = _mm(hp, w_in, norm_g=attn_norm[i], tn=768, name="in_proj").reshape(bp, t, W_IN_PAD)
        zs = _mm(hs, w_in, norm_g=attn_norm[i], tn=768, name="in_proj_dec").reshape(bs, ts, W_IN_PAD)
        if kind == 0:
            op, sp = _mixer_a(zp, f_bias, decode=False)
            os_, ss = _mixer_a(zs, f_bias, decode=True, caches=(cache_a_k, cache_a_v, cache_a_idx), slot=slot, page_table=page_table)
            key = "a"
        elif kind == 1:
            op, sp = _mixer_b(zp, b_kv_norm[slot], b_w_kvb[slot], decode=False)
            os_, ss = _mixer_b(zs, b_kv_norm[slot], b_w_kvb[slot], decode=True, caches=(cache_b_latent, cache_b_rope), slot=slot,
                               page_table=page_table)
            key = "b"
        else:
            op, sp = _mixer_c(zp, c_gate_b[slot], c_cmp_wk[slot], c_cmp_wv[slot], f_bias, decode=False)
            os_, ss = _mixer_c(zs, c_gate_b[slot], c_cmp_wk[slot], c_cmp_wv[slot], f_bias, decode=True,
                               caches=(cache_c_cmp_k, cache_c_cmp_v, cache_c_slc_k, cache_c_slc_v), slot=slot,
                               win=(state_c_win_k[slot], state_c_win_v[slot]), page_table=page_table)
            key = "c"
        st_p[key].append(sp)
        st_s[key].append(ss)
        hp = _mm(op, w_o, layer=slot, res=hp, name="out_proj")
        hs = _mm(os_, w_o, layer=slot, res=hs, name="out_proj_dec")
        fp, tail = _ffn_up(hp, ffn_norm[i], ffn_w_up, i, ffn_conv_w[i], ffn_conv_b[i], seq_len=t)
        fs, gs = _ffn_up(hs, ffn_norm[i], ffn_w_up, i, ffn_conv_w[i], ffn_conv_b[i], seq_len=ts, prev=state_ffn_conv[i])
        hp = _mm(fp, ffn_w_down, layer=i, res=hp, name="ffn_down")
        hs = _mm(fs, ffn_w_down, layer=i, res=hs, name="ffn_down_dec")
        blocks_per_seq = tail.shape[0] // bp
        st_p["conv"].append(tail[blocks_per_seq - 1::blocks_per_seq, V7X_SUBLANES - (CONV_W - 1):, :])
        st_s["conv"].append(gs.reshape(bs, ts, D_FF)[:, ts - (CONV_W - 1):])
    y_prompt = _rmsnorm(hp, final_norm).reshape(bp, t, d)
    y_sample = _rmsnorm(hs, final_norm).reshape(bs, ts, d)

    def stack(states, j, shape_tail):
        return jnp.stack([s[j].reshape(s[j].shape[:2] + shape_tail) for s in states])

    def group(st):
        kv = (A_KV_HEADS, HEAD_DIM)
        cg = (C_GROUPS, HEAD_DIM)
        return (stack(st["a"], 0, kv), stack(st["a"], 1, kv), stack(st["a"], 2, (IDX_DIM,)),
                stack(st["b"], 0, (B_LORA,)), stack(st["b"], 1, (B_ROPE,)),
                stack(st["c"], 0, cg), stack(st["c"], 1, cg), stack(st["c"], 2, cg), stack(st["c"], 3, cg),
                stack(st["c"], 4, cg), stack(st["c"], 5, cg), jnp.stack(st["conv"]))

    return (y_prompt, y_sample) + group(st_p) + group(st_s)
```

```python
import functools
import math

import jax
import jax.numpy as jnp
import numpy as np
from jax import lax
from jax.experimental import pallas as pl
from jax.experimental.pallas import tpu as pltpu

F32 = jnp.float32
BF16 = jnp.bfloat16
I32 = jnp.int32

D_MODEL = 2048
DEPTH = 4
PAGE_SIZE = 128
HEAD_DIM = 128
RMS_EPS = 1e-6
NEG = -1e30
FORCE_SCORE = 1e9
N_BUCKETS = 32
MAX_DISTANCE = 128
A_HEADS = 16
A_KV_HEADS = 2
IDX_HEADS = 16
IDX_DIM = 64
IDX_TOPK = 256
B_HEADS = 16
B_NOPE = 128
B_ROPE = 64
B_V = 128
B_LORA = 512
ROPE_THETA = 10000.0
C_HEADS = 16
C_GROUPS = 2
CMP_LEN = 32
CMP_STRIDE = 16
SLC_LEN = 64
SLC_TOPN = 16
WINDOW = 512
D_FF = 4096
CONV_W = 3

V7X_LANES = 128
V7X_SUBLANES = 8
V7X_VMEM_BYTES = 64 * 1024 * 1024
VMEM_LIMIT_BYTES = V7X_VMEM_BYTES - 8 * 1024 * 1024

W_IN_PAD = 3840
TAIL_COL = 3584
ROW_TILE_DEC = 8
MLA_QK = 640
BIAS_SAT = 113
MINF = -3.0e38


def _cparams(sem):
    return pltpu.CompilerParams(dimension_semantics=sem, vmem_limit_bytes=VMEM_LIMIT_BYTES)


def _round_up(x, m):
    return (x + m - 1) // m * m


def _mm_kernel(*refs, has_norm, has_res, cast_a):
    it = iter(refs)
    a_ref = next(it)
    g_ref = next(it) if has_norm else None
    w_ref = next(it)
    r_ref = next(it) if has_res else None
    o_ref = next(it)
    abf_ref = next(it) if cast_a else None
    if cast_a:
        @pl.when(pl.program_id(1) == 0)
        def _():
            a = a_ref[...].astype(F32)
            if has_norm:
                a = a * lax.rsqrt(jnp.mean(a * a, axis=-1, keepdims=True) + RMS_EPS) * g_ref[...]
            abf_ref[...] = a.astype(BF16)
        a = abf_ref[...]
    else:
        a = a_ref[...]
    acc = jnp.dot(a, w_ref[...].astype(BF16), preferred_element_type=F32)
    if has_res:
        acc = acc + r_ref[...]
    o_ref[...] = acc.astype(o_ref.dtype)


def _mm(a, w, *, layer=None, norm_g=None, res=None, tn=512, name="mm"):
    m, k = a.shape
    n = w.shape[-1]
    tm = min(m, 1024)
    assert m % tm == 0 and n % tn == 0
    has_norm, has_res = norm_g is not None, res is not None
    cast_a = has_norm or a.dtype != BF16
    ins = [a]
    specs = [pl.BlockSpec((tm, k), lambda i, j: (i, 0))]
    if has_norm:
        ins.append(norm_g.reshape(1, k))
        specs.append(pl.BlockSpec((1, k), lambda i, j: (0, 0)))
    ins.append(w)
    if layer is None:
        specs.append(pl.BlockSpec((k, tn), lambda i, j: (0, j)))
    else:
        specs.append(pl.BlockSpec((None, k, tn), lambda i, j: (layer, 0, j)))
    if has_res:
        ins.append(res)
        specs.append(pl.BlockSpec((tm, tn), lambda i, j: (i, j)))
    return pl.pallas_call(
        functools.partial(_mm_kernel, has_norm=has_norm, has_res=has_res, cast_a=cast_a),
        out_shape=jax.ShapeDtypeStruct((m, n), F32),
        grid=(m // tm, n // tn),
        in_specs=specs,
        out_specs=pl.BlockSpec((tm, tn), lambda i, j: (i, j)),
        scratch_shapes=[pltpu.VMEM((tm, k), BF16)] if cast_a else [],
        compiler_params=_cparams(("parallel", "arbitrary")),
        name=name,
    )(*ins)


def _rmsnorm_kernel(x_ref, g_ref, o_ref):
    x = x_ref[...]
    o_ref[...] = x * lax.rsqrt(jnp.mean(x * x, axis=-1, keepdims=True) + RMS_EPS) * g_ref[...]


def _rmsnorm(x, g):
    m, d = x.shape
    tm = min(m, 1024)
    return pl.pallas_call(
        _rmsnorm_kernel,
        out_shape=jax.ShapeDtypeStruct((m, d), F32),
        grid=(m // tm,),
        in_specs=[pl.BlockSpec((tm, d), lambda i: (i, 0)), pl.BlockSpec((1, d), lambda i: (0, 0))],
        out_specs=pl.BlockSpec((tm, d), lambda i: (i, 0)),
        compiler_params=_cparams(("parallel",)),
        name="final_norm",
    )(x, g.reshape(1, d))


def _ffn_up_kernel(*refs, tm, blocks_per_seq, decode):
    if decode:
        x_ref, gn_ref, wg_ref, wu_ref, cw_ref, cb_ref, p1_ref, p2_ref, h_ref, g_ref, xn_ref = refs
    else:
        x_ref, gn_ref, wg_ref, wu_ref, cw_ref, cb_ref, h_ref, tail_ref, xn_ref, carry_ref = refs
    i = pl.program_id(0)
    j = pl.program_id(1)

    @pl.when(j == 0)
    def _():
        x = x_ref[...]
        xn_ref[...] = (x * lax.rsqrt(jnp.mean(x * x, axis=-1, keepdims=True) + RMS_EPS) * gn_ref[...]).astype(BF16)

    xn = xn_ref[...]
    tf = wg_ref.shape[1]
    if not decode:
        @pl.when(i == 0)
        def _():
            carry_ref[j] = jnp.zeros((V7X_SUBLANES, tf), F32)

    half = tf // 2
    for part in range(2):
        cols = slice(part * half, (part + 1) * half)
        g = jnp.dot(xn, wg_ref[:, cols].astype(BF16), preferred_element_type=F32)
        u = jnp.dot(xn, wu_ref[:, cols].astype(BF16), preferred_element_type=F32)
        row = lax.broadcasted_iota(I32, g.shape, 0)
        r1 = pltpu.roll(g, 1, axis=0)
        r2 = pltpu.roll(g, 2, axis=0)
        if decode:
            t = row & 3
            gm1 = jnp.where(t >= 1, r1, p1_ref[:, cols])
            gm2 = jnp.where(t >= 2, r2, p2_ref[:, cols])
            g_ref[:, cols] = g
        else:
            c = jnp.where(i % blocks_per_seq == 0, 0.0, carry_ref[j, :, cols])
            c1 = c[V7X_SUBLANES - 1:V7X_SUBLANES, :]
            c2 = c[V7X_SUBLANES - 2:V7X_SUBLANES - 1, :]
            gm1 = jnp.where(row == 0, c1, r1)
            gm2 = jnp.where(row == 0, c2, jnp.where(row == 1, c1, r2))
            tail = g[tm - V7X_SUBLANES:, :]
            carry_ref[j, :, cols] = tail
            tail_ref[0, :, cols] = tail
        cw = cw_ref[:, cols]
        conv = cb_ref[:, cols] + gm2 * cw[0:1, :] + gm1 * cw[1:2, :] + g * cw[2:3, :]
        h = conv * (1.0 / (1.0 + jnp.exp(-conv))) * u
        h_ref[:, cols] = h.astype(BF16)


def _ffn_up(x, gn, w_up, layer, conv_w, conv_b, *, seq_len, prev=None, tf=512):
    m, d = x.shape
    nf = D_FF // tf
    decode = prev is not None
    tm = min(m, 1024)
    ins = [x, gn.reshape(1, d), w_up, w_up, conv_w, conv_b.reshape(1, D_FF)]
    specs = [
        pl.BlockSpec((tm, d), lambda i, j: (i, 0)),
        pl.BlockSpec((1, d), lambda i, j: (0, 0)),
        pl.BlockSpec((None, d, tf), lambda i, j: (layer, 0, j)),
        pl.BlockSpec((None, d, tf), lambda i, j: (layer, 0, j + nf)),
        pl.BlockSpec((CONV_W, tf), lambda i, j: (0, j)),
        pl.BlockSpec((1, tf), lambda i, j: (0, j)),
    ]
    if decode:
        assert seq_len == 4 and m == tm
        zero = jnp.zeros_like(prev[:, :1])
        p1 = jnp.concatenate([prev[:, 1:2], zero, zero, zero], axis=1).reshape(m, D_FF)
        p2 = jnp.concatenate([prev[:, 0:1], prev[:, 1:2], zero, zero], axis=1).reshape(m, D_FF)
        ins += [p1, p2]
        specs += [pl.BlockSpec((tm, tf), lambda i, j: (i, j))] * 2
        out_shape = (jax.ShapeDtypeStruct((m, D_FF), BF16), jax.ShapeDtypeStruct((m, D_FF), F32))
        out_specs = (pl.BlockSpec((tm, tf), lambda i, j: (i, j)), pl.BlockSpec((tm, tf), lambda i, j: (i, j)))
        scratch = [pltpu.VMEM((tm, d), BF16)]
        bps = 1
    else:
        assert seq_len % tm == 0
        bps = seq_len // tm
        out_shape = (jax.ShapeDtypeStruct((m, D_FF), BF16), jax.ShapeDtypeStruct((m // tm, V7X_SUBLANES, D_FF), F32))
        out_specs = (pl.BlockSpec((tm, tf), lambda i, j: (i, j)), pl.BlockSpec((1, V7X_SUBLANES, tf), lambda i, j: (i, 0, j)))
        scratch = [pltpu.VMEM((tm, d), BF16), pltpu.VMEM((nf, V7X_SUBLANES, tf), F32)]
    return pl.pallas_call(
        functools.partial(_ffn_up_kernel, tm=tm, blocks_per_seq=bps, decode=decode),
        out_shape=out_shape,
        grid=(m // tm, nf),
        in_specs=specs,
        out_specs=out_specs,
        scratch_shapes=scratch,
        compiler_params=_cparams(("arbitrary", "arbitrary")),
        name="ffn_up_dec" if decode else "ffn_up",
    )(*ins)


def _fa_chunk(i, kc, *, mode, tq, tk, qpos0, kpos0, n_chunks):
    if mode == "window":
        first = (qpos0 + i * tq - WINDOW + 1 - kpos0) // tk
        return first + kc
    return kc


def _fa_chunk_clamped(i, kc, **kw):
    c = _fa_chunk(i, kc, **kw)
    if kw["mode"] != "window":
        last = (kw["qpos0"] + i * kw["tq"] + kw["tq"] - 1 - kw["kpos0"]) // kw["tk"]
        c = jnp.minimum(c, last)
    return jnp.clip(c, 0, kw["n_chunks"] - 1)


def _fa_kernel(*refs, hg, dq, dv, tq, tk, nkc, scale, mode, qpos0, kpos0, n_chunks,
               has_mask, has_bias, has_vproj):
    it = iter(refs)
    q_ref, k_ref, v_ref = next(it), next(it), next(it)
    mask_ref = next(it) if has_mask else None
    bias_ref = next(it) if has_bias else None
    wv_ref = next(it) if has_vproj else None
    o_ref, qs_ref, m_ref, acc_ref = next(it), next(it), next(it), next(it)
    i = pl.program_id(1)
    n = pl.program_id(2)
    kc = pl.program_id(3)
    t0 = qpos0 + i * tq
    geo = dict(mode=mode, tq=tq, tk=tk, qpos0=qpos0, kpos0=kpos0, n_chunks=n_chunks)

    @pl.when(kc == 0)
    def _():
        for h in range(hg):
            qs_ref[h * tq:(h + 1) * tq, :] = q_ref[0, :, h * dq:(h + 1) * dq].astype(BF16)
        m_ref[...] = jnp.full(m_ref.shape, NEG, F32)
        acc_ref[...] = jnp.zeros(acc_ref.shape, F32)

    craw = _fa_chunk(i, kc, **geo)
    s0 = kpos0 + craw * tk
    needed = (craw >= 0) & (craw < n_chunks) & (s0 <= t0 + tq - 1)

    @pl.when(needed)
    def _():
        kb = k_ref[0].astype(BF16)
        s = lax.dot_general(qs_ref[...], kb, (((1,), (1,)), ((), ())), preferred_element_type=F32) * scale
        if has_bias:
            parts = []
            for u in range(tk // V7X_LANES):
                dsel = jnp.clip(t0 // V7X_LANES - (s0 // V7X_LANES + u), 0, 2)
                parts.append(bias_ref[n, dsel])
            s = s + (parts[0] if len(parts) == 1 else jnp.concatenate(parts, axis=1))
        add = None
        if has_mask:
            add = mask_ref[0, 0, 0]
        if mode in ("causal", "window"):
            qp = t0 + lax.broadcasted_iota(I32, (tq, tk), 0)
            kp = s0 + lax.broadcasted_iota(I32, (tq, tk), 1)
            ok = kp <= qp
            if mode == "window":
                ok = ok & (kp > qp - WINDOW) & (kp >= 0)
            add = jnp.where(ok, 0.0, NEG)
        if add is not None:
            s = s + (add if hg == 1 else jnp.concatenate([add] * hg, axis=0))
        m_old = m_ref[...]
        m_new = jnp.maximum(m_old, jnp.max(s, axis=1, keepdims=True))
        alpha = jnp.exp(m_old - m_new)
        p = jnp.exp(s - jnp.concatenate([m_new] * (tk // V7X_LANES), axis=1))
        vb = v_ref[0].astype(BF16)
        vext = jnp.concatenate([vb, jnp.ones((tk, V7X_LANES), BF16)], axis=1)
        pv = jnp.dot(p.astype(BF16), vext, preferred_element_type=F32)
        acc_ref[...] = acc_ref[...] * jnp.concatenate([alpha] * (dv // V7X_LANES + 1), axis=1) + pv
        m_ref[...] = m_new

    @pl.when(kc == nkc - 1)
    def _():
        valid = m_ref[...] > 0.5 * NEG
        l = jnp.where(valid, acc_ref[:, dv:dv + V7X_LANES], 1.0)
        inv = jnp.where(valid, 1.0 / l, 0.0)
        o = acc_ref[:, :dv] * jnp.concatenate([inv] * (dv // V7X_LANES), axis=1)
        for h in range(hg):
            oh = o[h * tq:(h + 1) * tq, :]
            if has_vproj:
                oh = jnp.dot(oh.astype(BF16), wv_ref[h], preferred_element_type=F32)
            o_ref[0, :, h * HEAD_DIM:(h + 1) * HEAD_DIM] = oh


def _fa(q, k, v, *, groups, hg, dq, dv, tq, tk, nkc, n_chunks, scale, mode, qpos0=0, kpos0=0,
        q_col=lambda n: n, k_col=lambda n: n, v_col=lambda n: n, mask=None, bias=None, vproj=None, name="fa"):
    b, tq_total = q.shape[0], q.shape[1]
    nq = tq_total // tq
    geo = dict(mode=mode, tq=tq, tk=tk, qpos0=qpos0, kpos0=kpos0, n_chunks=n_chunks)
    kmap = lambda col: (lambda bb, i, n, kc: (bb, _fa_chunk_clamped(i, kc, **geo), col(n)))
    ins = [q, k, v]
    specs = [
        pl.BlockSpec((1, tq, hg * dq), lambda bb, i, n, kc: (bb, i, q_col(n))),
        pl.BlockSpec((1, tk, dq), kmap(k_col)),
        pl.BlockSpec((1, tk, dv), kmap(v_col)),
    ]
    if mask is not None:
        ins.append(mask)
        gsel = (lambda n: n) if mask.shape[1] > 1 else (lambda n: 0)
        specs.append(pl.BlockSpec((1, 1, 1, tq, tk), lambda bb, i, n, kc: (bb, gsel(n), _fa_chunk_clamped(i, kc, **geo), i, 0)))
    if bias is not None:
        ins.append(bias)
        specs.append(pl.BlockSpec(bias.shape, lambda bb, i, n, kc: (0, 0, 0, 0)))
    if vproj is not None:
        ins.append(vproj)
        specs.append(pl.BlockSpec(vproj.shape, lambda bb, i, n, kc: (0, 0, 0)))
    r = hg * tq
    return pl.pallas_call(
        functools.partial(_fa_kernel, hg=hg, dq=dq, dv=dv, tq=tq, tk=tk, nkc=nkc, scale=scale, mode=mode,
                          qpos0=qpos0, kpos0=kpos0, n_chunks=n_chunks, has_mask=mask is not None,
                          has_bias=bias is not None, has_vproj=vproj is not None),
        out_shape=jax.ShapeDtypeStruct((b, tq_total, groups * hg * HEAD_DIM), F32),
        grid=(b, nq, groups, nkc),
        in_specs=specs,
        out_specs=pl.BlockSpec((1, tq, hg * HEAD_DIM), lambda bb, i, n, kc: (bb, i, n)),
        scratch_shapes=[pltpu.VMEM((r, dq), BF16), pltpu.VMEM((r, V7X_LANES), F32), pltpu.VMEM((r, dv + V7X_LANES), F32)],
        compiler_params=_cparams(("parallel", "parallel", "arbitrary", "arbitrary")),
        name=name,
    )(*ins)


def _sortable(x):
    b = lax.bitcast_convert_type(x, I32)
    return b ^ ((b >> 31) & jnp.int32(0x7FFFFFFF))


_NT = (((1,), (1,)), ((), ()))


def _select_scores(qstack, kic, wcols, rows, *, keys_minor=False):
    if keys_minor:
        d = jnp.dot(qstack, kic, preferred_element_type=F32)
    else:
        d = lax.dot_general(qstack, kic, _NT, preferred_element_type=F32)
    acc = None
    for h in range(IDX_HEADS):
        term = jnp.maximum(d[h * rows:(h + 1) * rows], 0.0) * wcols[h]
        acc = term if acc is None else acc + term
    return acc + 0.0


def _select_topk(key_ref, emit, *, rows, tk, nck, n_chunks, n_keep, qpos, live):
    lane = lax.broadcasted_iota(I32, (rows, tk), 1)
    nl = tk // V7X_LANES

    def count(pred):
        def body(kc, c):
            hit = jnp.where(pred(key_ref[kc], kc), 1.0, 0.0)
            part = hit[:, :V7X_LANES]
            for u in range(1, nl):
                part = part + hit[:, u * V7X_LANES:(u + 1) * V7X_LANES]
            return c + part
        total = lax.fori_loop(0, nck, body, jnp.zeros((rows, V7X_LANES), F32), unroll=isinstance(nck, int))
        return jnp.sum(total, axis=1, keepdims=True)

    neg_key = jnp.int32(np.array(NEG, np.float32).view(np.int32) ^ 0x7FFFFFFF)
    extra = jnp.asarray((n_chunks - nck) * tk).astype(F32)
    kf = float(n_keep)

    def count_ge(cand):
        return count(lambda kk, kc: kk >= cand) + jnp.where(cand <= neg_key, extra, 0.0)

    tau = jnp.where(count_ge(jnp.zeros((rows, 1), I32)) >= kf, 0, jnp.iinfo(jnp.int32).min).astype(I32)
    for bit in range(30, -1, -1):
        cand = tau | jnp.int32(1 << bit)
        tau = jnp.where(count_ge(cand) >= kf, cand, tau)
    n_ge = count_ge(tau)
    ties = jnp.max(jnp.where((n_ge != kf) & live, 1.0, 0.0)) > 0.5

    def write(sel_fn):
        def body(kc, carry):
            ok = sel_fn(key_ref[kc], kc) & (kc * tk + lane <= qpos)
            emit(kc, jnp.where(ok, 0.0, NEG))
            return carry
        lax.fori_loop(0, nck, body, 0)

        def fill(kc, carry):
            emit(kc, jnp.full((rows, tk), NEG, F32))
            return carry
        lax.fori_loop(nck, n_chunks, fill, 0)

    @pl.when(jnp.logical_not(ties))
    def _():
        write(lambda kk, kc: kk >= tau)

    @pl.when(ties)
    def _():
        r = kf - count(lambda kk, kc: kk > tau) - jnp.where(tau < neg_key, extra, 0.0)
        nbits = int(math.ceil(math.log2(n_chunks * tk)))
        mth = jnp.zeros((rows, 1), I32)
        for bit in range(nbits - 1, -1, -1):
            cand = mth | jnp.int32(1 << bit)
            c = count(lambda kk, kc: (kk == tau) & (kc * tk + lane < cand))
            c = c + jnp.where(tau == neg_key, jnp.clip(cand - nck * tk, 0, (n_chunks - nck) * tk).astype(F32), 0.0)
            mth = jnp.where(c < r, cand, mth)
        write(lambda kk, kc: (kk > tau) | ((kk == tau) & (kc * tk + lane <= mth)))


def _stack_index_queries(qs_ref, qa, qb, rows):
    hh = IDX_HEADS // 2
    for h in range(IDX_HEADS):
        src = qa if h < hh else qb
        qs_ref[h * rows:(h + 1) * rows, :] = src[:, (h % hh) * IDX_DIM:(h % hh + 1) * IDX_DIM].astype(BF16)


def _dsa_select_kernel(qa_ref, qb_ref, wi_ref, kidx_ref, o_ref, key_ref, qs_ref, *, tq, tk, n_chunks, n_keep):
    i = pl.program_id(1)
    t0 = i * tq
    nck = jnp.minimum((t0 + tq - 1) // tk + 1, n_chunks)
    _stack_index_queries(qs_ref, qa_ref[0], qb_ref[0], tq)
    w = wi_ref[0][:, IDX_DIM:IDX_DIM + IDX_HEADS] * (IDX_HEADS ** -0.5)
    wcols = [w[:, h:h + 1] for h in range(IDX_HEADS)]
    qpos = t0 + lax.broadcasted_iota(I32, (tq, tk), 0)
    lane = lax.broadcasted_iota(I32, (tq, tk), 1)

    def score_chunk(kc, carry):
        kic = kidx_ref[0, pl.ds(pl.multiple_of(kc * tk, tk), tk), :][:, :IDX_DIM].astype(BF16)
        sc = _select_scores(qs_ref[...], kic, wcols, tq)
        key_ref[kc] = _sortable(jnp.where(kc * tk + lane <= qpos, sc, NEG))
        return carry

    lax.fori_loop(0, nck, score_chunk, 0)

    def emit(kc, mask):
        o_ref[0, 0, kc] = mask

    _select_topk(key_ref, emit, rows=tq, tk=tk, nck=nck, n_chunks=n_chunks, n_keep=n_keep, qpos=qpos,
                 live=jnp.full((tq, 1), True))


def _dsa_select(z, *, tq, tk, n_keep):
    b, t = z.shape[0], z.shape[1]
    n_chunks = t // tk
    half = IDX_HEADS * IDX_DIM // 2
    qi_col = (A_HEADS * HEAD_DIM + 2 * A_KV_HEADS * HEAD_DIM) // half
    tail = TAIL_COL // V7X_LANES
    return pl.pallas_call(
        functools.partial(_dsa_select_kernel, tq=tq, tk=tk, n_chunks=n_chunks, n_keep=n_keep),
        out_shape=jax.ShapeDtypeStruct((b, 1, n_chunks, t, tk), F32),
        grid=(b, t // tq),
        in_specs=[
            pl.BlockSpec((1, tq, half), lambda bb, i: (bb, i, qi_col)),
            pl.BlockSpec((1, tq, half), lambda bb, i: (bb, i, qi_col + 1)),
            pl.BlockSpec((1, tq, V7X_LANES), lambda bb, i: (bb, i, tail)),
            pl.BlockSpec((1, t, V7X_LANES), lambda bb, i: (bb, 0, tail)),
        ],
        out_specs=pl.BlockSpec((1, 1, n_chunks, tq, tk), lambda bb, i: (bb, 0, 0, i, 0)),
        scratch_shapes=[pltpu.VMEM((n_chunks, tq, tk), I32), pltpu.VMEM((IDX_HEADS * tq, IDX_DIM), BF16)],
        compiler_params=_cparams(("parallel", "arbitrary")),
        name="dsa_select",
    )(z, z, z, z)


def _page_copies(pt_ref, pool_ref, dst, sem, *, slot, row, n_pages, rows_per_page):
    def at(p):
        return pltpu.make_async_copy(
            pool_ref.at[slot, pt_ref[row, p]],
            dst.at[pl.ds(pl.multiple_of(p * rows_per_page, rows_per_page), rows_per_page)], sem)
    return at


def _dsa_select_paged_kernel(pt_ref, qa_ref, qb_ref, wi_ref, knew_ref, pool_ref, o_ref, kbuf, sem, key_ref, qs_ref, *,
                             slot, tb, tk, n_pages, n_keep, past, n_steps):
    s = pl.program_id(0)
    tq = ROW_TILE_DEC
    rows = tb * tq
    nc_past = past // tk
    n_chunks = nc_past + 1

    ppc = tk // PAGE_SIZE

    def each_page(step, si, fn):
        for j in range(tb):
            def body(c, carry):
                for u in range(ppc):
                    page = pt_ref[step * tb + j, c * ppc + u]
                    fn(pltpu.make_async_copy(pool_ref.at[slot, page], kbuf.at[si, j, c, :, pl.ds(u * PAGE_SIZE, PAGE_SIZE)],
                                             sem.at[si]))
                return carry
            lax.fori_loop(0, nc_past, body, 0)

    @pl.when(s == 0)
    def _():
        each_page(0, 0, lambda c: c.start())

    @pl.when(s + 1 < n_steps)
    def _():
        each_page(s + 1, (s + 1) % 2, lambda c: c.start())

    si = s % 2
    each_page(s, si, lambda c: c.wait())

    lane_new = lax.broadcasted_iota(I32, (tq, PAGE_SIZE), 1)
    arow = lax.broadcasted_iota(I32, (tq, PAGE_SIZE), 0)
    for j in range(tb):
        _stack_index_queries(qs_ref, qa_ref[j], qb_ref[j], tq)
        w = wi_ref[j][:, IDX_DIM:IDX_DIM + IDX_HEADS] * (IDX_HEADS ** -0.5)
        wcols = [w[:, h:h + 1] for h in range(IDX_HEADS)]

        def score_chunk(kc, carry):
            kic = kbuf[si, j, kc].astype(BF16)
            key_ref[kc, j * tq:(j + 1) * tq, :] = _sortable(_select_scores(qs_ref[...], kic, wcols, tq, keys_minor=True))
            return carry

        lax.fori_loop(0, nc_past, score_chunk, 0)
        sc = _select_scores(qs_ref[...], knew_ref[j].astype(BF16), wcols, tq, keys_minor=True)
        sc = jnp.where(lane_new <= arow, sc, NEG)
        key_ref[nc_past, j * tq:(j + 1) * tq, :] = _sortable(
            jnp.concatenate([sc, jnp.full((tq, tk - PAGE_SIZE), NEG, F32)], axis=1))

    a = lax.broadcasted_iota(I32, (rows, tk), 0) % tq
    qpos = past + a

    def emit(kc, mask):
        for j in range(tb):
            o_ref[j, 0, kc] = mask[j * tq:(j + 1) * tq]

    live = lax.broadcasted_iota(I32, (rows, 1), 0) % tq < tq // 2
    _select_topk(key_ref, emit, rows=rows, tk=tk, nck=n_chunks, n_chunks=n_chunks, n_keep=n_keep, qpos=qpos, live=live)


def _dsa_select_paged(z8, knew, pool, page_table, *, slot, tb, tk, n_keep):
    b = z8.shape[0]
    n_pages = page_table.shape[1]
    past = n_pages * PAGE_SIZE
    n_chunks = past // tk + 1
    tq = ROW_TILE_DEC
    half = IDX_HEADS * IDX_DIM // 2
    qi_col = (A_HEADS * HEAD_DIM + 2 * A_KV_HEADS * HEAD_DIM) // half
    tail = TAIL_COL // V7X_LANES
    n_steps = b // tb
    grid_spec = pltpu.PrefetchScalarGridSpec(
        num_scalar_prefetch=1,
        grid=(n_steps,),
        in_specs=[
            pl.BlockSpec((tb, tq, half), lambda s, pt: (s, 0, qi_col)),
            pl.BlockSpec((tb, tq, half), lambda s, pt: (s, 0, qi_col + 1)),
            pl.BlockSpec((tb, tq, V7X_LANES), lambda s, pt: (s, 0, tail)),
            pl.BlockSpec((tb, IDX_DIM, PAGE_SIZE), lambda s, pt: (s, 0, 0)),
            pl.BlockSpec(memory_space=pl.ANY),
        ],
        out_specs=pl.BlockSpec((tb, 1, n_chunks, tq, tk), lambda s, pt: (s, 0, 0, 0, 0)),
        scratch_shapes=[
            pltpu.VMEM((2, tb, past // tk, IDX_DIM, tk), F32),
            pltpu.SemaphoreType.DMA((2,)),
            pltpu.VMEM((n_chunks, tb * tq, tk), I32),
            pltpu.VMEM((IDX_HEADS * tq, IDX_DIM), BF16),
        ],
    )
    return pl.pallas_call(
        functools.partial(_dsa_select_paged_kernel, slot=slot, tb=tb, tk=tk, n_pages=n_pages, n_keep=n_keep, past=past,
                          n_steps=n_steps),
        out_shape=jax.ShapeDtypeStruct((b, 1, n_chunks, tq, tk), F32),
        grid_spec=grid_spec,
        compiler_params=_cparams(("arbitrary",)),
        name="dsa_select_dec",
    )(page_table, z8, z8, z8, knew, pool)


def _paged_attn_kernel(pt_ref, q_ref, knew_ref, vnew_ref, pk_ref, pv_ref, *rest, kind, slot, nc, nb, depth, past, scale,
                       mask_groups, has_bias, has_vproj):
    it = iter(rest)
    mask_ref = next(it) if mask_groups else None
    bias_ref = next(it) if has_bias else None
    wv_ref = next(it) if has_vproj else None
    o_ref, kbuf, vbuf, sem, qs_ref, m_ref, acc_ref = (next(it) for _ in range(7))
    b = pl.program_id(0)
    tq = ROW_TILE_DEC
    groups, hg = (A_KV_HEADS, A_HEADS // A_KV_HEADS) if kind == "gqa" else (1, B_HEADS)
    rpp = PAGE_SIZE * groups
    tk = kbuf.shape[1] // groups
    ppc = tk // PAGE_SIZE
    dv = HEAD_DIM if kind == "gqa" else B_LORA
    total = nb * nc

    def copies(g):
        bb = g // nc
        cc = g - bb * nc
        si = g % depth
        out = []
        for u in range(ppc):
            page = pt_ref[bb, cc * ppc + u]
            rows = pl.ds(u * rpp, rpp)
            out.append(pltpu.make_async_copy(pk_ref.at[slot, page], kbuf.at[si, rows], sem.at[si, 0]))
            if kind == "gqa":
                out.append(pltpu.make_async_copy(pv_ref.at[slot, page], vbuf.at[si, rows], sem.at[si, 1]))
            else:
                out.append(pltpu.make_async_copy(pv_ref.at[slot, page], vbuf.at[si, :, pl.ds(u * PAGE_SIZE, PAGE_SIZE)],
                                                 sem.at[si, 1]))
        return out

    @pl.when(b == 0)
    def _():
        for g in range(depth - 1):
            for c in copies(g):
                c.start()

    for n in range(groups):
        for h in range(hg):
            if kind == "gqa":
                qs_ref[n, h * tq:(h + 1) * tq, :] = q_ref[0, :, (n * hg + h) * HEAD_DIM:(n * hg + h + 1) * HEAD_DIM].astype(BF16)
            else:
                qs_ref[n, h * tq:(h + 1) * tq, :] = q_ref[0, :, h * MLA_QK:(h + 1) * MLA_QK]
    m_ref[...] = jnp.full(m_ref.shape, NEG, F32)
    acc_ref[...] = jnp.zeros(acc_ref.shape, F32)

    def update(n, s, vb):
        keys = s.shape[1]
        m_old = m_ref[n]
        m_new = jnp.maximum(m_old, jnp.max(s, axis=1, keepdims=True))
        alpha = jnp.exp(m_old - m_new)
        p = jnp.exp(s - jnp.concatenate([m_new] * (keys // V7X_LANES), axis=1))
        if kind == "gqa":
            vext = jnp.concatenate([vb, jnp.ones((keys, V7X_LANES), BF16)], axis=1)
            pv = jnp.dot(p.astype(BF16), vext, preferred_element_type=F32)
        else:
            rsum = jnp.broadcast_to(jnp.sum(p, axis=1, keepdims=True), (p.shape[0], V7X_LANES))
            pv = jnp.concatenate([jnp.dot(p.astype(BF16), vb, preferred_element_type=F32), rsum], axis=1)
        acc_ref[n] = acc_ref[n] * jnp.concatenate([alpha] * (dv // V7X_LANES + 1), axis=1) + pv
        m_ref[n] = m_new

    def attend(kget, vget, add, bias_sel):
        for n in range(groups):
            kb = kget(n).astype(BF16)
            vb = vget(n).astype(BF16)
            if kind == "gqa":
                s = lax.dot_general(qs_ref[n], kb, _NT, preferred_element_type=F32)
                vn = vb
            else:
                q = qs_ref[n]
                s = (lax.dot_general(q[:, :B_LORA], kb, _NT, preferred_element_type=F32)
                     + jnp.dot(q[:, B_LORA:B_LORA + B_ROPE], vb, preferred_element_type=F32))
                vn = kb
            s = s * scale
            if has_bias:
                parts = [bias_ref[n, d] for d in bias_sel]
                s = s + (parts[0] if len(parts) == 1 else jnp.concatenate(parts, axis=1))
            a = add(n)
            if a is not None:
                s = s + jnp.concatenate([a] * hg, axis=0)
            update(n, s, vn)

    def chunk(c, carry):
        g = b * nc + c
        nxt = g + depth - 1

        @pl.when(nxt < total)
        def _():
            for cp in copies(nxt):
                cp.start()

        for cp in copies(g):
            cp.wait()
        si = g % depth
        qtile = past // V7X_LANES
        bias_sel = [jnp.clip(qtile - (c * ppc + u), 0, 2) for u in range(ppc)]
        add = (lambda n: mask_ref[0, n if mask_groups > 1 else 0, c]) if mask_groups else (lambda n: None)
        if kind == "gqa":
            attend(lambda n: kbuf.at[si][pl.ds(n, tk, stride=groups), :], lambda n: vbuf.at[si][pl.ds(n, tk, stride=groups), :],
                   add, bias_sel)
        else:
            attend(lambda n: kbuf[si], lambda n: vbuf[si], add, bias_sel)
        return carry

    lax.fori_loop(0, nc, chunk, 0)

    if mask_groups:
        add_new = lambda n: mask_ref[0, n if mask_groups > 1 else 0, nc][:, :PAGE_SIZE]
    else:
        kp = lax.broadcasted_iota(I32, (tq, PAGE_SIZE), 1)
        qp = lax.broadcasted_iota(I32, (tq, PAGE_SIZE), 0)
        causal = jnp.where(kp <= qp, 0.0, NEG)
        add_new = lambda n: causal
    if kind == "gqa":
        cols = lambda n: slice(n * HEAD_DIM, (n + 1) * HEAD_DIM)
        attend(lambda n: knew_ref[0][:, cols(n)], lambda n: vnew_ref[0][:, cols(n)], add_new, [0])
    else:
        attend(lambda n: knew_ref[0], lambda n: vnew_ref[0], add_new, [0])

    for n in range(groups):
        valid = m_ref[n] > 0.5 * NEG
        acc = acc_ref[n]
        l = jnp.where(valid, acc[:, dv:dv + V7X_LANES], 1.0)
        inv = jnp.where(valid, 1.0 / l, 0.0)
        o = acc[:, :dv] * jnp.concatenate([inv] * (dv // V7X_LANES), axis=1)
        for h in range(hg):
            oh = o[h * tq:(h + 1) * tq, :]
            if has_vproj:
                oh = jnp.dot(oh.astype(BF16), wv_ref[h], preferred_element_type=F32)
            o_ref[0, :, (n * hg + h) * HEAD_DIM:(n * hg + h + 1) * HEAD_DIM] = oh


def _paged_attn(q, knew, vnew, pool_k, pool_v, page_table, *, kind, slot, scale, tk=512, depth=8, mask=None, bias=None, vproj=None,
                name="paged_attn"):
    b = q.shape[0]
    n_pages = page_table.shape[1]
    nc = n_pages * PAGE_SIZE // tk
    past = n_pages * PAGE_SIZE
    assert b * nc >= depth
    tq = ROW_TILE_DEC
    if kind == "gqa":
        groups, hg, dq, dv, qw = A_KV_HEADS, A_HEADS // A_KV_HEADS, HEAD_DIM, HEAD_DIM, A_HEADS * HEAD_DIM
    else:
        groups, hg, dq, dv, qw = 1, B_HEADS, MLA_QK, B_LORA, B_HEADS * MLA_QK
    wk = pool_k.shape[-1]
    assert pool_k.shape[2] == PAGE_SIZE * groups
    if kind == "gqa":
        assert pool_v.shape[2:] == (PAGE_SIZE * groups, HEAD_DIM)
        vbuf_shape = (depth, tk * groups, HEAD_DIM)
    else:
        assert pool_v.shape[2:] == (B_ROPE, PAGE_SIZE)
        vbuf_shape = (depth, B_ROPE, tk)
    ins = [page_table, q, knew, vnew, pool_k, pool_v]
    specs = [
        pl.BlockSpec((1, tq, qw), lambda i, pt: (i, 0, 0)),
        pl.BlockSpec((1,) + knew.shape[1:], lambda i, pt: (i, 0, 0)),
        pl.BlockSpec((1,) + vnew.shape[1:], lambda i, pt: (i, 0, 0)),
        pl.BlockSpec(memory_space=pl.ANY),
        pl.BlockSpec(memory_space=pl.ANY),
    ]
    mask_groups = 0
    if mask is not None:
        mask_groups = mask.shape[1]
        ins.append(mask)
        specs.append(pl.BlockSpec((1,) + mask.shape[1:], lambda i, pt: (i, 0, 0, 0, 0)))
    if bias is not None:
        ins.append(bias)
        specs.append(pl.BlockSpec(bias.shape, lambda i, pt: (0, 0, 0, 0)))
    if vproj is not None:
        ins.append(vproj)
        specs.append(pl.BlockSpec(vproj.shape, lambda i, pt: (0, 0, 0)))
    r = hg * tq
    grid_spec = pltpu.PrefetchScalarGridSpec(
        num_scalar_prefetch=1,
        grid=(b,),
        in_specs=specs,
        out_specs=pl.BlockSpec((1, tq, groups * hg * HEAD_DIM), lambda i, pt: (i, 0, 0)),
        scratch_shapes=[
            pltpu.VMEM((depth, tk * groups, wk), F32),
            pltpu.VMEM(vbuf_shape, F32),
            pltpu.SemaphoreType.DMA((depth, 2)),
            pltpu.VMEM((groups, r, dq), BF16),
            pltpu.VMEM((groups, r, V7X_LANES), F32),
            pltpu.VMEM((groups, r, dv + V7X_LANES), F32),
        ],
    )
    return pl.pallas_call(
        functools.partial(_paged_attn_kernel, kind=kind, slot=slot, nc=nc, nb=b, depth=depth, past=past, scale=scale,
                          mask_groups=mask_groups, has_bias=bias is not None, has_vproj=vproj is not None),
        out_shape=jax.ShapeDtypeStruct((b, tq, groups * hg * HEAD_DIM), F32),
        grid_spec=grid_spec,
        compiler_params=_cparams(("arbitrary",)),
        name=name,
    )(*ins)


def _rope_rows(x, cos, sin):
    half = x.shape[-1] // 2
    x1, x2 = x[:, :half], x[:, half:]
    return jnp.concatenate([x1 * cos - x2 * sin, x1 * sin + x2 * cos], axis=-1)


def _mla_prep_kernel(qf_ref, c_ref, kr_ref, gk_ref, wuk_ref, cos_ref, sin_ref, qcat_ref, ckr_ref, cout_ref, krout_ref):
    cos, sin = cos_ref[...], sin_ref[...]
    tq = cos.shape[0]
    c = c_ref[0]
    cn = c * lax.rsqrt(jnp.mean(c * c, axis=-1, keepdims=True) + RMS_EPS) * gk_ref[...]
    kr = _rope_rows(kr_ref[0][:, :B_ROPE], cos, sin)
    cout_ref[0] = cn
    krout_ref[0] = kr
    zpad = jnp.zeros((tq, MLA_QK - B_LORA - B_ROPE), BF16)
    ckr_ref[0] = jnp.concatenate([cn.astype(BF16), kr.astype(BF16), zpad], axis=1)
    hd = B_NOPE + B_ROPE
    for h in range(B_HEADS):
        qn = qf_ref[0, :, h * hd:h * hd + B_NOPE].astype(BF16)
        qr = _rope_rows(qf_ref[0, :, h * hd + B_NOPE:(h + 1) * hd], cos, sin)
        ql = jnp.dot(qn, wuk_ref[h], preferred_element_type=F32)
        qcat_ref[0, :, h * MLA_QK:(h + 1) * MLA_QK] = jnp.concatenate([ql.astype(BF16), qr.astype(BF16), zpad], axis=1)


def _mla_prep(z, kv_norm, wuk_t, cos, sin, *, tq):
    b, t = z.shape[0], z.shape[1]
    qw = B_HEADS * (B_NOPE + B_ROPE)
    half = B_ROPE // 2
    return pl.pallas_call(
        _mla_prep_kernel,
        out_shape=(
            jax.ShapeDtypeStruct((b, t, B_HEADS * MLA_QK), BF16),
            jax.ShapeDtypeStruct((b, t, MLA_QK), BF16),
            jax.ShapeDtypeStruct((b, t, B_LORA), F32),
            jax.ShapeDtypeStruct((b, t, B_ROPE), F32),
        ),
        grid=(b, t // tq),
        in_specs=[
            pl.BlockSpec((1, tq, qw), lambda bb, i: (bb, i, 0)),
            pl.BlockSpec((1, tq, B_LORA), lambda bb, i: (bb, i, qw // B_LORA)),
            pl.BlockSpec((1, tq, V7X_LANES), lambda bb, i: (bb, i, TAIL_COL // V7X_LANES)),
            pl.BlockSpec((1, B_LORA), lambda bb, i: (0, 0)),
            pl.BlockSpec(wuk_t.shape, lambda bb, i: (0, 0, 0)),
            pl.BlockSpec((tq, half), lambda bb, i: (i, 0)),
            pl.BlockSpec((tq, half), lambda bb, i: (i, 0)),
        ],
        out_specs=(
            pl.BlockSpec((1, tq, B_HEADS * MLA_QK), lambda bb, i: (bb, i, 0)),
            pl.BlockSpec((1, tq, MLA_QK), lambda bb, i: (bb, i, 0)),
            pl.BlockSpec((1, tq, B_LORA), lambda bb, i: (bb, i, 0)),
            pl.BlockSpec((1, tq, B_ROPE), lambda bb, i: (bb, i, 0)),
        ),
        compiler_params=_cparams(("parallel", "parallel")),
        name="mla_prep",
    )(z, z, z, kv_norm.reshape(1, B_LORA), wuk_t, cos, sin)


def _compress_lines(x, w1_ref, w2_ref, o_ref, *, n_sub, n_out):
    gw = C_GROUPS * HEAD_DIM
    first = jnp.zeros((n_sub, gw), F32)
    second = jnp.zeros((n_sub, gw), F32)
    for r in range(CMP_STRIDE):
        xr = x(r)
        first = first + xr * w1_ref[:, r * gw:(r + 1) * gw]
        second = second + xr * w2_ref[:, r * gw:(r + 1) * gw]
    blocks = (first + pltpu.roll(second, n_sub - 1, axis=0)) * (1.0 / CMP_LEN)
    o_ref[0, :n_sub, :] = blocks
    if n_out > n_sub:
        o_ref[0, n_sub:, :] = jnp.zeros((n_out - n_sub, gw), F32)


def _nsa_compress_kernel(xk_ref, xv_ref, wk1_ref, wk2_ref, wv1_ref, wv2_ref, ok_ref, ov_ref, *, n_sub, n_out):
    gw = C_GROUPS * HEAD_DIM
    for x_ref, w1_ref, w2_ref, o_ref in ((xk_ref, wk1_ref, wk2_ref, ok_ref), (xv_ref, wv1_ref, wv2_ref, ov_ref)):
        _compress_lines(lambda r, x_ref=x_ref: x_ref[0, :, r * gw:(r + 1) * gw], w1_ref, w2_ref, o_ref, n_sub=n_sub, n_out=n_out)


def _flat_cmp_weights(w):
    width = CMP_STRIDE * C_GROUPS * HEAD_DIM
    w2 = jnp.broadcast_to(w.reshape(2, CMP_STRIDE, 1, HEAD_DIM), (2, CMP_STRIDE, C_GROUPS, HEAD_DIM))
    w2 = w2.reshape(2, 1, width)
    return w2[0], w2[1]


LINE_ROWS = CMP_STRIDE * C_GROUPS


def _nsa_compress_paged_kernel(pt_ref, nk_ref, nv_ref, pk_ref, pv_ref, wk_ref, wv_ref, ok_ref, ov_ref,
                               xk, xv, sem, t1_ref, t2_ref, *, slot, n_pages, n_sub, n_out, n_steps, n_parts):
    s = pl.program_id(0)
    rpp = PAGE_SIZE * C_GROUPS
    tiles = LINE_ROWS // V7X_SUBLANES
    sub = V7X_SUBLANES

    def each_page(step, si, fn):
        for a, (pool, buf) in enumerate(((pk_ref, xk), (pv_ref, xv))):
            cp = _page_copies(pt_ref, pool, buf.at[si], sem.at[si, a], slot=slot, row=step, n_pages=n_pages, rows_per_page=rpp)

            def body(p, carry):
                fn(cp(p))
                return carry
            lax.fori_loop(0, n_pages, body, 0)

    @pl.when(s == 0)
    def _():
        each_page(0, 0, lambda c: c.start())

    @pl.when(s + 1 < n_steps)
    def _():
        each_page(s + 1, (s + 1) % 2, lambda c: c.start())

    si = s % 2
    each_page(s, si, lambda c: c.wait())
    paged = n_pages * rpp
    lines = n_sub // n_parts
    for buf, n_ref, w_ref, o_ref in ((xk, nk_ref, wk_ref, ok_ref), (xv, nv_ref, wv_ref, ov_ref)):
        buf[si, paged:, :] = n_ref[0]
        w1 = w_ref[0].reshape(1, tiles, sub, HEAD_DIM)
        w2 = w_ref[1].reshape(1, tiles, sub, HEAD_DIM)
        for part in range(n_parts):
            x = buf[si, part * lines * LINE_ROWS:(part + 1) * lines * LINE_ROWS, :].reshape(lines, tiles, sub, HEAD_DIM)
            rows = slice(part * lines * sub, (part + 1) * lines * sub)
            t1_ref[rows, :] = jnp.sum(x * w1, axis=1).reshape(lines * sub, HEAD_DIM)
            t2_ref[rows, :] = jnp.sum(x * w2, axis=1).reshape(lines * sub, HEAD_DIM)
        for g in range(C_GROUPS):
            first = second = None
            for j in range(sub // C_GROUPS):
                r = pl.ds(C_GROUPS * j + g, n_sub, stride=sub)
                first = t1_ref[r, :] if first is None else first + t1_ref[r, :]
                second = t2_ref[r, :] if second is None else second + t2_ref[r, :]
            blocks = (first + pltpu.roll(second, n_sub - 1, axis=0)) * (1.0 / CMP_LEN)
            o_ref[0, :n_sub, g * HEAD_DIM:(g + 1) * HEAD_DIM] = blocks
        if n_out > n_sub:
            o_ref[0, n_sub:, :] = jnp.zeros((n_out - n_sub, C_GROUPS * HEAD_DIM), F32)


def _nsa_compress_paged(new_k, new_v, pool_k, pool_v, page_table, wk, wv, *, slot, n_out):
    b = new_k.shape[0]
    n_pages = page_table.shape[1]
    rpp = PAGE_SIZE * C_GROUPS
    n_rows = n_pages * rpp + new_k.shape[1]
    n_sub = n_rows // LINE_ROWS
    n_parts = 4 if n_sub % 4 == 0 else 1
    tile_w = lambda w: jnp.repeat(w.reshape(2, CMP_STRIDE, HEAD_DIM), C_GROUPS, axis=1)
    wspec = pl.BlockSpec((2, LINE_ROWS, HEAD_DIM), lambda s, pt: (0, 0, 0))
    nspec = pl.BlockSpec((1, new_k.shape[1], HEAD_DIM), lambda s, pt: (s, 0, 0))
    ospec = pl.BlockSpec((1, n_out, C_GROUPS * HEAD_DIM), lambda s, pt: (s, 0, 0))
    grid_spec = pltpu.PrefetchScalarGridSpec(
        num_scalar_prefetch=1,
        grid=(b,),
        in_specs=[nspec, nspec, pl.BlockSpec(memory_space=pl.ANY), pl.BlockSpec(memory_space=pl.ANY), wspec, wspec],
        out_specs=(ospec, ospec),
        scratch_shapes=[pltpu.VMEM((2, n_rows, HEAD_DIM), F32), pltpu.VMEM((2, n_rows, HEAD_DIM), F32),
                        pltpu.SemaphoreType.DMA((2, 2)),
                        pltpu.VMEM((n_sub * V7X_SUBLANES, HEAD_DIM), F32), pltpu.VMEM((n_sub * V7X_SUBLANES, HEAD_DIM), F32)],
    )
    return pl.pallas_call(
        functools.partial(_nsa_compress_paged_kernel, slot=slot, n_pages=n_pages, n_sub=n_sub, n_out=n_out, n_steps=b,
                          n_parts=n_parts),
        out_shape=(jax.ShapeDtypeStruct((b, n_out, C_GROUPS * HEAD_DIM), F32),) * 2,
        grid_spec=grid_spec,
        compiler_params=_cparams(("arbitrary",)),
        name="nsa_compress_dec",
    )(page_table, new_k, new_v, pool_k, pool_v, tile_w(wk), tile_w(wv))


def _nsa_compress(xk, xv, wk, wv, *, n_out):
    b, n_sub, width = xk.shape
    wk1, wk2 = _flat_cmp_weights(wk)
    wv1, wv2 = _flat_cmp_weights(wv)
    wspec = pl.BlockSpec((1, width), lambda bb: (0, 0))
    return pl.pallas_call(
        functools.partial(_nsa_compress_kernel, n_sub=n_sub, n_out=n_out),
        out_shape=(jax.ShapeDtypeStruct((b, n_out, C_GROUPS * HEAD_DIM), F32),) * 2,
        grid=(b,),
        in_specs=[pl.BlockSpec((1, n_sub, width), lambda bb: (bb, 0, 0))] * 2 + [wspec] * 4,
        out_specs=(pl.BlockSpec((1, n_out, C_GROUPS * HEAD_DIM), lambda bb: (bb, 0, 0)),) * 2,
        compiler_params=_cparams(("parallel",)),
        name="nsa_compress",
    )(xk, xv, wk1, wk2, wv1, wv2)


def _nsa_cmp_kernel(q_ref, kcb_ref, vcb_ref, bias_ref, m_ref, e_ref, o_ref, mask_ref, *, tq, tk, nb, ns_pad, n_slc, n_top, tkeys, qpos0):
    i = pl.program_id(1)
    t0 = qpos0 + i * tq
    hg = C_HEADS // C_GROUPS
    scale = HEAD_DIM ** -0.5
    r = hg * tq
    qrow = t0 + lax.broadcasted_iota(I32, (tq, nb), 0)
    cend = lax.broadcasted_iota(I32, (tq, nb), 1) * CMP_STRIDE + (CMP_LEN - 1)
    vis = jnp.where(cend <= qrow, 0.0, NEG)
    qs = t0 + lax.broadcasted_iota(I32, (tq, ns_pad), 0)
    blk = lax.broadcasted_iota(I32, (tq, ns_pad), 1)
    forced = (blk == qs // SLC_LEN) | (blk == 0)
    admissible = blk * SLC_LEN <= qs
    qk = t0 + lax.broadcasted_iota(I32, (tq, tkeys), 0)
    kpos = lax.broadcasted_iota(I32, (tq, tkeys), 1)
    for g in range(C_GROUPS):
        qg = jnp.concatenate([q_ref[0, :, (g * hg + h) * HEAD_DIM:(g * hg + h + 1) * HEAD_DIM] for h in range(hg)],
                             axis=0).astype(BF16)
        kg = kcb_ref[0, :, g * HEAD_DIM:(g + 1) * HEAD_DIM].astype(BF16)
        vg = vcb_ref[0, :, g * HEAD_DIM:(g + 1) * HEAD_DIM].astype(BF16)
        s = lax.dot_general(qg, kg, (((1,), (1,)), ((), ())), preferred_element_type=F32) * scale
        s = s + bias_ref[g, 0] + jnp.concatenate([vis] * hg, axis=0)
        mx = jnp.max(s, axis=1, keepdims=True)
        e = jnp.exp(s - mx)
        valid = mx > 0.5 * NEG
        p = e * jnp.where(valid, 1.0 / jnp.sum(e, axis=1, keepdims=True), 0.0)
        o = jnp.dot(p.astype(BF16), vg, preferred_element_type=F32)
        psum = p[0:tq]
        for h in range(hg):
            o_ref[0, :, (g * hg + h) * HEAD_DIM:(g * hg + h + 1) * HEAD_DIM] = o[h * tq:(h + 1) * tq]
            if h:
                psum = psum + p[h * tq:(h + 1) * tq]
        p_hi = psum.astype(BF16)
        p_lo = (psum - p_hi.astype(F32)).astype(BF16)
        imp = jnp.dot(p_hi, m_ref[...], preferred_element_type=F32) + jnp.dot(p_lo, m_ref[...], preferred_element_type=F32)
        imp = jnp.where(forced, FORCE_SCORE, jnp.where(admissible, imp, NEG))
        imp = jnp.where(blk < n_slc, imp, MINF)
        rank = jnp.zeros((tq, ns_pad), F32)
        for c in range(n_slc):
            col = imp[:, c:c + 1]
            rank = rank + jnp.where((col > imp) | ((col == imp) & (blk > c)), 1.0, 0.0)
        sel = jnp.where(rank < n_top, 1.0, 0.0).astype(BF16)
        hit = jnp.dot(sel, e_ref[...], preferred_element_type=F32)
        msk = jnp.where((hit > 0.5) & (kpos <= qk), 0.0, NEG)
        for c in range(tkeys // tk):
            mask_ref[0, g, c] = msk[:, c * tk:(c + 1) * tk]


def _nsa_cmp(q, kcb, vcb, bias_c, *, tq, tk, n_slc, tkeys, qpos0, q_col=0):
    b, tq_total = q.shape[0], q.shape[1]
    nb = kcb.shape[1]
    ns_pad = _round_up(n_slc, V7X_LANES)
    c0 = np.arange(nb)[:, None] * CMP_STRIDE
    s0 = np.arange(ns_pad)[None, :] * SLC_LEN
    shared = np.minimum(c0 + CMP_LEN, s0 + SLC_LEN) - np.maximum(c0, s0)
    m = np.maximum(shared, 0).astype(np.float32) / CMP_LEN
    m[:, n_slc:] = 0.0
    e = (np.arange(tkeys)[None, :] // SLC_LEN == np.arange(ns_pad)[:, None]).astype(np.float32)
    hq = C_HEADS * HEAD_DIM
    return pl.pallas_call(
        functools.partial(_nsa_cmp_kernel, tq=tq, tk=tk, nb=nb, ns_pad=ns_pad, n_slc=n_slc, n_top=min(SLC_TOPN, n_slc),
                          tkeys=tkeys, qpos0=qpos0),
        out_shape=(jax.ShapeDtypeStruct((b, tq_total, hq), F32), jax.ShapeDtypeStruct((b, C_GROUPS, tkeys // tk, tq_total, tk), F32)),
        grid=(b, tq_total // tq),
        in_specs=[
            pl.BlockSpec((1, tq, hq), lambda bb, i: (bb, i, q_col)),
            pl.BlockSpec((1, nb, C_GROUPS * HEAD_DIM), lambda bb, i: (bb, 0, 0)),
            pl.BlockSpec((1, nb, C_GROUPS * HEAD_DIM), lambda bb, i: (bb, 0, 0)),
            pl.BlockSpec((C_GROUPS, 1, (C_HEADS // C_GROUPS) * tq, nb), lambda bb, i: (0, i, 0, 0)),
            pl.BlockSpec((nb, ns_pad), lambda bb, i: (0, 0)),
            pl.BlockSpec((ns_pad, tkeys), lambda bb, i: (0, 0)),
        ],
        out_specs=(
            pl.BlockSpec((1, tq, hq), lambda bb, i: (bb, i, 0)),
            pl.BlockSpec((1, C_GROUPS, tkeys // tk, tq, tk), lambda bb, i: (bb, 0, 0, i, 0)),
        ),
        compiler_params=_cparams(("parallel", "parallel")),
        name="nsa_cmp",
    )(q, kcb, vcb, bias_c, jnp.asarray(m, BF16), jnp.asarray(e, BF16))


def _nsa_combine_kernel(oc_ref, os_ref, ow_ref, g_ref, gb_ref, o_ref):
    g = g_ref[...][:, :3 * C_HEADS] + gb_ref[...]
    gate = 1.0 / (1.0 + jnp.exp(-g))
    for h in range(C_HEADS):
        sl = slice(h * HEAD_DIM, (h + 1) * HEAD_DIM)
        o_ref[:, sl] = (oc_ref[:, sl] * gate[:, 3 * h:3 * h + 1] + os_ref[:, sl] * gate[:, 3 * h + 1:3 * h + 2]
                        + ow_ref[:, sl] * gate[:, 3 * h + 2:3 * h + 3])


def _nsa_combine(o_cmp, o_slc, o_win, z, gate_b):
    m, d = o_cmp.shape
    tm = min(m, 512)
    ospec = pl.BlockSpec((tm, d), lambda i: (i, 0))
    return pl.pallas_call(
        _nsa_combine_kernel,
        out_shape=jax.ShapeDtypeStruct((m, d), F32),
        grid=(m // tm,),
        in_specs=[ospec, ospec, ospec, pl.BlockSpec((tm, V7X_LANES), lambda i: (i, TAIL_COL // V7X_LANES)),
                  pl.BlockSpec((1, 3 * C_HEADS), lambda i: (0, 0))],
        out_specs=ospec,
        compiler_params=_cparams(("parallel",)),
        name="nsa_combine",
    )(o_cmp, o_slc, o_win, z, gate_b.reshape(1, 3 * C_HEADS))


def _t5_bucket(dist):
    n = jnp.maximum(dist, 0)
    exact = N_BUCKETS // 2
    nf = jnp.maximum(n, 1).astype(F32)
    large = exact + (jnp.log(nf / exact) / math.log(MAX_DISTANCE / exact) * (N_BUCKETS - exact)).astype(I32)
    return jnp.where(n < exact, n, jnp.minimum(large, N_BUCKETS - 1))


def _bias_by_distance(rel_bias):
    return rel_bias[_t5_bucket(jnp.arange(2 * V7X_LANES))]


def _bias_tiles(f, tq, groups):
    a = jnp.arange(tq)[:, None]
    c = jnp.arange(V7X_LANES)[None, :]
    t0 = f[jnp.clip(a - c, 0, 2 * V7X_LANES - 1)]
    t1 = f[V7X_LANES + a - c]
    t2 = jnp.broadcast_to(f[2 * V7X_LANES - 1], t0.shape)
    tiles = jnp.stack([t0, t1, t2], axis=0)
    heads = f.shape[1]
    tiles = tiles.transpose(3, 0, 1, 2).reshape(groups, heads // groups, 3, tq, V7X_LANES)
    return tiles.transpose(0, 2, 1, 3, 4).reshape(groups, 3, (heads // groups) * tq, V7X_LANES)


def _bias_cmp(f, qpos0, nq, nb, tq, groups):
    heads = f.shape[1]
    hg = heads // groups
    shift = tq // CMP_STRIDE if nq > 1 else 0
    assert nq == 1 or tq % CMP_STRIDE == 0
    lead = shift * (nq - 1)
    a = jnp.arange(tq)[:, None]
    c = jnp.arange(nb + lead)[None, :] - lead
    g0 = f[jnp.clip(qpos0 + a - (c * CMP_STRIDE + CMP_LEN - 1), 0, 2 * V7X_LANES - 1)]
    g0 = g0.reshape(tq, nb + lead, groups, hg).transpose(2, 3, 0, 1).reshape(groups, hg * tq, nb + lead)
    return jnp.stack([g0[:, :, lead - shift * i:lead - shift * i + nb] for i in range(nq)], axis=1)


def _rope_tables(pos):
    half = B_ROPE // 2
    inv = ROPE_THETA ** (-jnp.arange(half, dtype=F32) / half)
    ang = pos.astype(F32)[:, None] * inv[None, :]
    return jnp.cos(ang), jnp.sin(ang)


def _pad_cols(w):
    return jnp.pad(w, ((0, 0), (0, W_IN_PAD - w.shape[1])))


def _pad_rows(x, rows):
    return jnp.pad(x, ((0, 0), (0, rows - x.shape[1])) + ((0, 0),) * (x.ndim - 2))


TK = 512
TK_DEC = 1024
SELECT_BATCH_DEC = 4


def _positions_minor(x):
    return jnp.swapaxes(x, -1, -2)


def _merge_groups(pool):
    return pool.reshape(pool.shape[:2] + (-1, pool.shape[-1]))


def _mixer_a(z, f_bias, *, decode, caches=None, slot=0, page_table=None):
    b, tq_total = z.shape[0], z.shape[1]
    kcol = A_HEADS * HEAD_DIM
    k_new = z[:, :, kcol:kcol + A_KV_HEADS * HEAD_DIM]
    v_new = z[:, :, kcol + A_KV_HEADS * HEAD_DIM:kcol + 2 * A_KV_HEADS * HEAD_DIM]
    ki_new = z[:, :, TAIL_COL:TAIL_COL + IDX_DIM]
    hg = A_HEADS // A_KV_HEADS
    if not decode:
        tq = V7X_LANES
        t = tq_total
        n_chunks = t // TK
        mask = _dsa_select(z, tq=tq, tk=TK, n_keep=min(IDX_TOPK, t // 4))
        bias = _bias_tiles(f_bias, tq, A_KV_HEADS)
        o = _fa(z, z, z, groups=A_KV_HEADS, hg=hg, dq=HEAD_DIM, dv=HEAD_DIM, tq=tq, tk=TK, nkc=n_chunks,
                n_chunks=n_chunks, scale=HEAD_DIM ** -0.5, mode="mask",
                k_col=lambda n: kcol // HEAD_DIM + n, v_col=lambda n: kcol // HEAD_DIM + A_KV_HEADS + n,
                mask=mask, bias=bias, name="dsa_attn")
    else:
        cache_k, cache_v, cache_idx = caches
        past = page_table.shape[1] * PAGE_SIZE
        tq = ROW_TILE_DEC
        z8 = _pad_rows(z, tq)
        mask = _dsa_select_paged(z8, _positions_minor(_pad_rows(ki_new, PAGE_SIZE)), _positions_minor(cache_idx), page_table,
                                 slot=slot, tb=SELECT_BATCH_DEC, tk=TK_DEC, n_keep=min(IDX_TOPK, (past + tq_total) // 4))
        bias = _bias_tiles(f_bias, tq, A_KV_HEADS)
        o = _paged_attn(z8, _pad_rows(k_new, PAGE_SIZE), _pad_rows(v_new, PAGE_SIZE), _merge_groups(cache_k), _merge_groups(cache_v),
                        page_table, kind="gqa", slot=slot, scale=HEAD_DIM ** -0.5, tk=TK_DEC, mask=mask, bias=bias, name="dsa_attn_dec")
        o = o[:, :tq_total]
    return o.reshape(b * tq_total, A_HEADS * HEAD_DIM), (k_new, v_new, ki_new)


def _mixer_b(z, kv_norm, w_kvb, *, decode, caches=None, slot=0, page_table=None):
    b, tq_total = z.shape[0], z.shape[1]
    w = w_kvb.reshape(B_LORA, B_HEADS, B_NOPE + B_V)
    wuk_t = w[..., :B_NOPE].transpose(1, 2, 0).astype(BF16)
    wuv = w[..., B_NOPE:].transpose(1, 0, 2).astype(BF16)
    scale = (B_NOPE + B_ROPE) ** -0.5
    if not decode:
        tq = V7X_LANES
        cos, sin = _rope_tables(jnp.arange(tq_total))
        qcat, ckr, c_new, kr_new = _mla_prep(z, kv_norm, wuk_t, cos, sin, tq=tq)
        n_chunks = tq_total // TK
        o = _fa(qcat, ckr, ckr, groups=1, hg=B_HEADS, dq=MLA_QK, dv=B_LORA, tq=tq, tk=TK, nkc=n_chunks, n_chunks=n_chunks,
                scale=scale, mode="causal", q_col=lambda n: 0, k_col=lambda n: 0, v_col=lambda n: 0, vproj=wuv, name="mla_attn")
    else:
        cache_lat, cache_rope = caches
        past = page_table.shape[1] * PAGE_SIZE
        tq = ROW_TILE_DEC
        z8 = _pad_rows(z, tq)
        cos, sin = _rope_tables(past + jnp.arange(tq))
        qcat, ckr8, c8, kr8 = _mla_prep(z8, kv_norm, wuk_t, cos, sin, tq=tq)
        c_new, kr_new = c8[:, :tq_total], kr8[:, :tq_total]
        o = _paged_attn(qcat, _pad_rows(c_new, PAGE_SIZE), _positions_minor(_pad_rows(kr_new, PAGE_SIZE)), cache_lat,
                        _positions_minor(cache_rope), page_table, kind="mla", slot=slot, scale=scale, tk=TK_DEC, vproj=wuv,
                        name="mla_attn_dec")
        o = o[:, :tq_total]
    return o.reshape(b * tq_total, B_HEADS * B_V), (c_new, kr_new)


def _mixer_c(z, gate_b, wk_pos, wv_pos, f_bias, *, decode, caches=None, slot=0, win=None, page_table=None):
    b, tq_total = z.shape[0], z.shape[1]
    gw = C_GROUPS * HEAD_DIM
    q0 = C_HEADS * HEAD_DIM
    kc, vc, ks, vs, kw, vw = (z[:, :, q0 + j * gw:q0 + (j + 1) * gw] for j in range(6))
    hg = C_HEADS // C_GROUPS
    scale = HEAD_DIM ** -0.5
    line = CMP_STRIDE * gw
    if not decode:
        t = tq_total
        tq = V7X_LANES
        n_sub = t // CMP_STRIDE
        kcb, vcb = _nsa_compress(kc.reshape(b, n_sub, line), vc.reshape(b, n_sub, line), wk_pos, wv_pos, n_out=n_sub)
        n_slc = -(-t // SLC_LEN)
        bias_c = _bias_cmp(f_bias, 0, t // tq, n_sub, tq, C_GROUPS)
        o_cmp, mask = _nsa_cmp(z, kcb, vcb, bias_c, tq=tq, tk=TK, n_slc=n_slc, tkeys=t, qpos0=0)
        bias = _bias_tiles(f_bias, tq, C_GROUPS)
        n_chunks = t // TK
        blk = lambda j: (lambda n: (q0 + j * gw) // HEAD_DIM + n)
        o_slc = _fa(z, z, z, groups=C_GROUPS, hg=hg, dq=HEAD_DIM, dv=HEAD_DIM, tq=tq, tk=TK, nkc=n_chunks, n_chunks=n_chunks,
                    scale=scale, mode="mask", k_col=blk(2), v_col=blk(3), mask=mask, bias=bias, name="nsa_slc")
        tkw = 2 * V7X_LANES
        o_win = _fa(z, z, z, groups=C_GROUPS, hg=hg, dq=HEAD_DIM, dv=HEAD_DIM, tq=tq, tk=tkw, nkc=WINDOW // tkw + 1,
                    n_chunks=t // tkw, scale=scale, mode="window", k_col=blk(4), v_col=blk(5), bias=bias, name="nsa_win")
        n_w = min(WINDOW, t)
        win_new = (kw[:, t - n_w:], vw[:, t - n_w:])
        zrows = z.reshape(b * t, W_IN_PAD)
    else:
        cache_kc, cache_vc, cache_ks, cache_vs = caches
        win_k, win_v = win
        past = page_table.shape[1] * PAGE_SIZE
        tq = ROW_TILE_DEC
        z8 = _pad_rows(z, tq)
        n_sub = past // CMP_STRIDE + V7X_SUBLANES
        new_rows = lambda x: _pad_rows(x.reshape(b, tq_total * C_GROUPS, HEAD_DIM), V7X_SUBLANES * LINE_ROWS)
        nb = _round_up(n_sub, V7X_LANES)
        kcb, vcb = _nsa_compress_paged(new_rows(kc), new_rows(vc), _merge_groups(cache_kc), _merge_groups(cache_vc), page_table,
                                       wk_pos, wv_pos, slot=slot, n_out=nb)
        n_slc = -(-(past + tq_total) // SLC_LEN)
        tkeys = past + TK_DEC
        bias_c = _bias_cmp(f_bias, past, 1, nb, tq, C_GROUPS)
        o_cmp, mask = _nsa_cmp(z8, kcb, vcb, bias_c, tq=tq, tk=TK_DEC, n_slc=n_slc, tkeys=tkeys, qpos0=past)
        bias = _bias_tiles(f_bias, tq, C_GROUPS)
        o_slc = _paged_attn(z8, _pad_rows(ks, PAGE_SIZE), _pad_rows(vs, PAGE_SIZE), _merge_groups(cache_ks), _merge_groups(cache_vs),
                            page_table, kind="gqa", slot=slot, scale=scale, tk=TK_DEC, mask=mask, bias=bias, name="nsa_slc_dec")
        n_buf = win_k.shape[1]
        wrows = _round_up(n_buf + tq, V7X_LANES)
        tkw = wrows
        kw_cat = jnp.concatenate([win_k.reshape(b, n_buf, gw), kw], axis=1)
        vw_cat = jnp.concatenate([win_v.reshape(b, n_buf, gw), vw], axis=1)
        o_win = _fa(z8, _pad_rows(kw_cat, wrows), _pad_rows(vw_cat, wrows), groups=C_GROUPS, hg=hg, dq=HEAD_DIM, dv=HEAD_DIM,
                    tq=tq, tk=tkw, nkc=wrows // tkw, n_chunks=wrows // tkw, scale=scale, mode="window", qpos0=past,
                    kpos0=past - n_buf, bias=bias, name="nsa_win_dec")
        n_w = min(WINDOW, n_buf + tq_total)
        win_new = (kw_cat[:, -n_w:], vw_cat[:, -n_w:])
        o_cmp, o_slc, o_win = (x[:, :tq_total] for x in (o_cmp, o_slc, o_win))
        zrows = z.reshape(b * tq_total, W_IN_PAD)
    hq = C_HEADS * HEAD_DIM
    o = _nsa_combine(o_cmp.reshape(-1, hq), o_slc.reshape(-1, hq), o_win.reshape(-1, hq), zrows, gate_b)
    return o, (kc, vc, ks, vs) + win_new


def kernel(x_prompt, x_sample, cache_a_k, cache_a_v, cache_a_idx, cache_b_latent, cache_b_rope, cache_c_cmp_k, cache_c_cmp_v, cache_c_slc_k, cache_c_slc_v, state_c_win_k, state_c_win_v, state_ffn_conv, page_table, rel_bias, attn_norm, ffn_norm, final_norm, a_w_in, a_w_o, b_w_in, b_kv_norm, b_w_kvb, b_w_o, c_w_in, c_gate_b, c_cmp_wk, c_cmp_wv, c_w_o, ffn_w_up, ffn_conv_w, ffn_conv_b, ffn_w_down):
    bp, t, d = x_prompt.shape
    bs, ts, _ = x_sample.shape
    hp = x_prompt.reshape(bp * t, d)
    hs = x_sample.reshape(bs * ts, d)
    f_bias = _bias_by_distance(rel_bias)
    st_p = {"a": [], "b": [], "c": [], "conv": []}
    st_s = {"a": [], "b": [], "c": [], "conv": []}
    for i in range(DEPTH):
        kind, slot = i % 3, i // 3
        w_in = _pad_cols((a_w_in, b_w_in, c_w_in)[kind][slot])
        w_o = (a_w_o, b_w_o, c_w_o)[kind]
        zp = _mm(hp, w_in, norm_g=attn_norm[i], tn=768, name="in_proj").reshape(bp, t, W_IN_PAD)
        zs = _mm(hs, w_in, norm_g=attn_norm[i], tn=768, name="in_proj_dec").reshape(bs, ts, W_IN_PAD)
        if kind == 0:
            op, sp = _mixer_a(zp, f_bias, decode=False)
            os_, ss = _mixer_a(zs, f_bias, decode=True, caches=(cache_a_k, cache_a_v, cache_a_idx), slot=slot, page_table=page_table)
            key = "a"
        elif kind == 1:
            op, sp = _mixer_b(zp, b_kv_norm[slot], b_w_kvb[slot], decode=False)
            os_, ss = _mixer_b(zs, b_kv_norm[slot], b_w_kvb[slot], decode=True, caches=(cache_b_latent, cache_b_rope), slot=slot,
                               page_table=page_table)
            key = "b"
        else:
            op, sp = _mixer_c(zp, c_gate_b[slot], c_cmp_wk[slot], c_cmp_wv[slot], f_bias, decode=False)
            os_, ss = _mixer_c(zs, c_gate_b[slot], c_cmp_wk[slot], c_cmp_wv[slot], f_bias, decode=True,
                               caches=(cache_c_cmp_k, cache_c_cmp_v, cache_c_slc_k, cache_c_slc_v), slot=slot,
                               win=(state_c_win_k[slot], state_c_win_v[slot]), page_table=page_table)
            key = "c"
        st_p[key].append(sp)
        st_s[key].append(ss)
        hp = _mm(op, w_o, layer=slot, res=hp, name="out_proj")
        hs = _mm(os_, w_o, layer=slot, res=hs, name="out_proj_dec")
        fp, tail = _ffn_up(hp, ffn_norm[i], ffn_w_up, i, ffn_conv_w[i], ffn_conv_b[i], seq_len=t)
        fs, gs = _ffn_up(hs, ffn_norm[i], ffn_w_up, i, ffn_conv_w[i], ffn_conv_b[i], seq_len=ts, prev=state_ffn_conv[i])
        hp = _mm(fp, ffn_w_down, layer=i, res=hp, name="ffn_down")
        hs = _mm(fs, ffn_w_down, layer=i, res=hs, name="ffn_down_dec")
        blocks_per_seq = tail.shape[0] // bp
        st_p["conv"].append(tail[blocks_per_seq - 1::blocks_per_seq, V7X_SUBLANES - (CONV_W - 1):, :])
        st_s["conv"].append(gs.reshape(bs, ts, D_FF)[:, ts - (CONV_W - 1):])
    y_prompt = _rmsnorm(hp, final_norm).reshape(bp, t, d)
    y_sample = _rmsnorm(hs, final_norm).reshape(bs, ts, d)

    def stack(states, j, shape_tail):
        return jnp.stack([s[j].reshape(s[j].shape[:2] + shape_tail) for s in states])

    def group(st):
        kv = (A_KV_HEADS, HEAD_DIM)
        cg = (C_GROUPS, HEAD_DIM)
        return (stack(st["a"], 0, kv), stack(st["a"], 1, kv), stack(st["a"], 2, (IDX_DIM,)),
                stack(st["b"], 0, (B_LORA,)), stack(st["b"], 1, (B_ROPE,)),
                stack(st["c"], 0, cg), stack(st["c"], 1, cg), stack(st["c"], 2, cg), stack(st["c"], 3, cg),
                stack(st["c"], 4, cg), stack(st["c"], 5, cg), jnp.stack(st["conv"]))

    return (y_prompt, y_sample) + group(st_p) + group(st_s)
```
